```python
import jax, jax.numpy as jnp
from jax import lax
import numpy as np

D_MODEL = 4096
BATCH = 16
SEQ = 256
DEPTH = 2
DEC_BATCH = 2
DEC_SEQ = 2048
PAST_LEN = 256

GRID_W = 64
W_A = D_MODEL // 2
W_B = D_MODEL // 2
W_C = D_MODEL
HEAD_C = 64
H_C = W_C // HEAD_C
CONV_A = 31
CONV_B = 3
SHIFT_C = 3
DECAY_LORA = 128
AAA_LORA = 128
GATE_LORA = 480
D_FF = 11008
N_MOD = 9
RMS_EPS = 1e-6
LN_EPS = 1e-5
GN_EPS = HEAD_C * 1e-5
DECAY_SCALE = 0.6065306597126334
SPLITS = (2 * W_A, 3 * W_B, 3 * W_C, 2 * DECAY_LORA, 2 * AAA_LORA, GATE_LORA, 3 * D_MODEL)
P_IN = sum(SPLITS)

kernel_name = "hybrid_conv_rwkv7_dit_step"


def rmsnorm(x, g):
    xf = x.astype(jnp.float32)
    y = xf * lax.rsqrt(jnp.mean(xf * xf, axis=-1, keepdims=True) + RMS_EPS)
    return (y * g).astype(x.dtype)


def layernorm(x, g, b):
    xf = x.astype(jnp.float32)
    mu = jnp.mean(xf, axis=-1, keepdims=True)
    var = jnp.mean(jnp.square(xf - mu), axis=-1, keepdims=True)
    return ((xf - mu) * lax.rsqrt(var + LN_EPS) * g + b).astype(x.dtype)


def modulate(h, shift, scale):
    return h * (1.0 + scale[:, None, :]) + shift[:, None, :]


def adaln(cond, w, b):
    return (jax.nn.silu(cond) @ w + b).reshape(cond.shape[0], N_MOD, D_MODEL)


def dwconv(x, w):
    k = w.shape[0]
    return lax.conv_general_dilated(
        x, w[:, None, :].astype(x.dtype), window_strides=(1,), padding=[(k // 2, k // 2)],
        dimension_numbers=('NWC', 'WIO', 'NWC'), feature_group_count=x.shape[-1])


def conv_rows(x, w, is_latent):
    if not is_latent:
        return dwconv(x, w)
    b, l, c = x.shape
    rows = l // GRID_W
    return dwconv(x.reshape(b * rows, GRID_W, c), w).reshape(b, l, c)


def conv_cols(x, w, is_latent):
    if not is_latent:
        return dwconv(x, w)
    b, l, c = x.shape
    rows = l // GRID_W
    xg = x.reshape(b, rows, GRID_W, c).transpose(0, 2, 1, 3).reshape(b * GRID_W, rows, c)
    y = dwconv(xg, w)
    return y.reshape(b, GRID_W, rows, c).transpose(0, 2, 1, 3).reshape(b, l, c)


def split_cols(z):
    offs = []
    acc = 0
    for s in SPLITS[:-1]:
        acc += s
        offs.append(acc)
    return jnp.split(z, offs, axis=-1)


def ffn_half(x, g, shift, scale, gate, w_gate, w_up, w_down):
    h = modulate(rmsnorm(x, g), shift, scale)
    f = (jax.nn.silu(h @ w_gate) * (h @ w_up)) @ w_down
    return x + 0.5 * gate[:, None, :] * f


def conformer_conv(za, is_latent, w_conv, ln_g, ln_b):
    u, gt = jnp.split(za, 2, axis=-1)
    y = conv_rows(u * jax.nn.sigmoid(gt), w_conv, is_latent)
    return jax.nn.silu(layernorm(y, ln_g, ln_b))


def short_conv(zb, is_latent, w_conv):
    xin, bg, cg = jnp.split(zb, 3, axis=-1)
    return bg * conv_cols(cg * xin, w_conv, is_latent)


def wkv_scan(s0, r, w, k, v, kk, a, reverse):
    def step(s, inp):
        r_t, w_t, k_t, v_t, kk_t, a_t = inp
        sa = jnp.einsum('bhvk,bhk->bhv', s, kk_t)
        s = (s * w_t[:, :, None, :] - sa[..., None] * (kk_t * a_t)[:, :, None, :]
             + v_t[..., None] * k_t[:, :, None, :])
        return s, jnp.einsum('bhvk,bhk->bhv', s, r_t)
    xs = tuple(jnp.moveaxis(t, 1, 0) for t in (r, w, k, v, kk, a))
    s_final, ys = lax.scan(step, s0, xs, reverse=reverse)
    return s_final, jnp.moveaxis(ys, 0, 1)


def rwkv7_bidir(zrkv, lw, la, lg, s0, w_shift, k_k, k_a, r_k, w0, w_w2, a0, w_a2, w_g2, gn_g, gn_b):
    b, l, _ = zrkv.shape
    dt = zrkv.dtype
    f32 = jnp.float32
    rkv = dwconv(zrkv, w_shift).astype(f32)
    r, k, v = jnp.split(rkv, 3, axis=-1)
    lw = jnp.tanh(lw.astype(f32)).reshape(b, l, 2, DECAY_LORA)
    la = la.astype(f32).reshape(b, l, 2, AAA_LORA)
    decay = jnp.exp(-DECAY_SCALE * jax.nn.sigmoid(w0 + jnp.einsum('bldr,drc->bldc', lw, w_w2)))
    lr = jax.nn.sigmoid(a0 + jnp.einsum('bldr,drc->bldc', la, w_a2))
    gate = jax.nn.sigmoid(lg) @ w_g2

    def heads(t):
        return t.reshape(b, l, H_C, HEAD_C)

    kk = heads(k * k_k)
    kk = kk / jnp.maximum(jnp.linalg.norm(kk, axis=-1, keepdims=True), 1e-12)
    rh, vh = heads(r), heads(v)
    ys, bonuses, finals = [], [], []
    for d in range(2):
        a_d = lr[:, :, d]
        k_d = heads(k * (1.0 + (a_d - 1.0) * k_a))
        s_d, y_d = wkv_scan(s0[:, d].astype(f32), rh, heads(decay[:, :, d]), k_d, vh, kk,
                            heads(a_d), reverse=(d == 1))
        ys.append(y_d)
        bonuses.append(jnp.sum(rh * k_d * r_k, axis=-1, keepdims=True) * vh)
        finals.append(s_d)
    y = ys[0] + ys[1]
    mu = jnp.mean(y, axis=-1, keepdims=True)
    var = jnp.mean(jnp.square(y - mu), axis=-1, keepdims=True)
    yn = ((y - mu) * lax.rsqrt(var + GN_EPS)).reshape(b, l, W_C) * gn_g + gn_b
    yn = yn + (bonuses[0] + bonuses[1]).reshape(b, l, W_C)
    return (yn * gate).astype(dt), jnp.stack(finals, axis=1)


def trunk_layer(x, mod, s0, is_latent, norm_g, w_ffn_gate, w_ffn_up, w_ffn_down, w_in,
                w_conv_a, ln_a_g, ln_a_b, w_out_a, w_conv_b, w_out_b, w_shift, k_k, k_a, r_k,
                w0, w_w2, a0, w_a2, w_g2, gn_g, gn_b, w_out_c, w_o):
    x = ffn_half(x, norm_g[0], mod[:, 0], mod[:, 1], mod[:, 2], w_ffn_gate[0], w_ffn_up[0], w_ffn_down[0])
    h = modulate(rmsnorm(x, norm_g[1]), mod[:, 3], mod[:, 4])
    z = h @ w_in
    za, zb, zrkv, lw, la, lg, zg = split_cols(z)
    pa = conformer_conv(za, is_latent, w_conv_a, ln_a_g, ln_a_b) @ w_out_a
    pb = short_conv(zb, is_latent, w_conv_b) @ w_out_b
    yc, s_final = rwkv7_bidir(zrkv, lw, la, lg, s0, w_shift, k_k, k_a, r_k, w0, w_w2, a0, w_a2,
                              w_g2, gn_g, gn_b)
    pc = yc @ w_out_c
    g_a, g_b, g_c = jnp.split(jax.nn.sigmoid(zg), 3, axis=-1)
    merged = g_a * pa + g_b * pb + g_c * pc
    x = x + mod[:, 5][:, None, :] * (merged @ w_o)
    x = ffn_half(x, norm_g[2], mod[:, 6], mod[:, 7], mod[:, 8], w_ffn_gate[1], w_ffn_up[1], w_ffn_down[1])
    return x, s_final


def setup_inputs(seed: int = 0) -> dict:
    key = jax.random.key(seed)
    ks = jax.random.split(key, 32)
    nrm = jax.random.normal
    f32 = jnp.float32
    d = D_MODEL
    return {
        "x_prompt": nrm(ks[0], (BATCH, SEQ, d), f32),
        "x_sample": nrm(ks[1], (DEC_BATCH, DEC_SEQ, d), f32),
        "state_rwkv": 0.3 * nrm(ks[2], (DEC_BATCH, DEPTH, 2, H_C, HEAD_C, HEAD_C), f32),
        "c": nrm(ks[3], (DEC_BATCH, d), f32),
        "c_ctx": nrm(ks[4], (d,), f32),
        "w_ada": nrm(ks[5], (DEPTH, d, N_MOD * d), f32) * (0.5 * d ** -0.5),
        "b_ada": 0.02 * nrm(ks[6], (DEPTH, N_MOD * d), f32),
        "norm_g": 1.0 + 0.05 * nrm(ks[7], (DEPTH, 3, d), f32),
        "w_ffn_gate": nrm(ks[8], (DEPTH, 2, d, D_FF), f32) * d ** -0.5,
        "w_ffn_up": nrm(ks[9], (DEPTH, 2, d, D_FF), f32) * d ** -0.5,
        "w_ffn_down": nrm(ks[10], (DEPTH, 2, D_FF, d), f32) * D_FF ** -0.5,
        "w_in": nrm(ks[11], (DEPTH, d, P_IN), f32) * d ** -0.5,
        "w_conv_a": nrm(ks[12], (DEPTH, CONV_A, W_A), f32) * CONV_A ** -0.5,
        "ln_a_g": 1.0 + 0.05 * nrm(ks[13], (DEPTH, W_A), f32),
        "ln_a_b": 0.02 * nrm(ks[14], (DEPTH, W_A), f32),
        "w_out_a": nrm(ks[15], (DEPTH, W_A, d), f32) * W_A ** -0.5,
        "w_conv_b": nrm(ks[16], (DEPTH, CONV_B, W_B), f32) * CONV_B ** -0.5,
        "w_out_b": nrm(ks[17], (DEPTH, W_B, d), f32) * W_B ** -0.5,
        "w_shift": nrm(ks[18], (DEPTH, SHIFT_C, 3 * W_C), f32) * SHIFT_C ** -0.5,
        "k_k": 1.0 + 0.1 * nrm(ks[19], (DEPTH, W_C), f32),
        "k_a": 1.0 + 0.1 * nrm(ks[20], (DEPTH, W_C), f32),
        "r_k": 0.1 * nrm(ks[21], (DEPTH, H_C, HEAD_C), f32),
        "w0": 0.5 * nrm(ks[22], (DEPTH, 2, W_C), f32),
        "w_w2": nrm(ks[23], (DEPTH, 2, DECAY_LORA, W_C), f32) * (0.5 * DECAY_LORA ** -0.5),
        "a0": 0.5 * nrm(ks[24], (DEPTH, 2, W_C), f32),
        "w_a2": nrm(ks[25], (DEPTH, 2, AAA_LORA, W_C), f32) * (0.5 * AAA_LORA ** -0.5),
        "w_g2": nrm(ks[26], (DEPTH, GATE_LORA, W_C), f32) * GATE_LORA ** -0.5,
        "gn_g": 1.0 + 0.05 * nrm(ks[27], (DEPTH, W_C), f32),
        "gn_b": 0.02 * nrm(ks[28], (DEPTH, W_C), f32),
        "w_out_c": nrm(ks[29], (DEPTH, W_C, d), f32) * W_C ** -0.5,
        "w_o": nrm(ks[30], (DEPTH, d, d), f32) * d ** -0.5,
        "final_g": 1.0 + 0.05 * nrm(ks[31], (d,), f32),
    }


def reference(x_prompt, x_sample, state_rwkv, c, c_ctx, w_ada, b_ada, norm_g, w_ffn_gate, w_ffn_up,
              w_ffn_down, w_in, w_conv_a, ln_a_g, ln_a_b, w_out_a, w_conv_b, w_out_b, w_shift, k_k,
              k_a, r_k, w0, w_w2, a0, w_a2, w_g2, gn_g, gn_b, w_out_c, w_o, final_g):
    xp = x_prompt
    xs = x_sample
    s_zero = jnp.zeros((x_prompt.shape[0], 2, H_C, HEAD_C, HEAD_C), jnp.float32)
    new_states = []
    for l in range(DEPTH):
        layer_w = (norm_g[l], w_ffn_gate[l], w_ffn_up[l], w_ffn_down[l], w_in[l], w_conv_a[l],
                   ln_a_g[l], ln_a_b[l], w_out_a[l], w_conv_b[l], w_out_b[l], w_shift[l], k_k[l],
                   k_a[l], r_k[l], w0[l], w_w2[l], a0[l], w_a2[l], w_g2[l], gn_g[l], gn_b[l],
                   w_out_c[l], w_o[l])
        mod_ctx = adaln(c_ctx[None, :], w_ada[l], b_ada[l])
        mod_lat = adaln(c, w_ada[l], b_ada[l])
        xp, s_ctx = trunk_layer(xp, mod_ctx, s_zero, False, *layer_w)
        new_states.append(s_ctx)
        xs, _ = trunk_layer(xs, mod_lat, state_rwkv[:, l], True, *layer_w)
    y_prompt = rmsnorm(xp, final_g)
    y_sample = rmsnorm(xs, final_g)
    new_state_rwkv = jnp.stack(new_states, axis=1)
    return (y_prompt, y_sample, new_state_rwkv)
```

```python
import functools

import jax
import jax.numpy as jnp
from jax import lax
from jax.experimental import pallas as pl
from jax.experimental.pallas import tpu as pltpu

F32 = jnp.float32
BF16 = jnp.bfloat16

GRID_W = 64
HEAD = 64
N_MOD = 9
RMS_EPS = 1e-6
LN_EPS = 1e-5
GN_EPS = HEAD * 1e-5
DECAY_SCALE = 0.6065306597126334
LANES = 128
SUBLANES = 8
VMEM_LIMIT = 56 * 1024 * 1024


def _pick(n, prefs):
    for p in prefs:
        if n % p == 0:
            return p
    raise ValueError(f"no tile in {prefs} divides {n}")


def _params(sem):
    return pltpu.CompilerParams(dimension_semantics=sem, vmem_limit_bytes=VMEM_LIMIT)


def _sigmoid(x):
    return 1.0 / (1.0 + jnp.exp(-x))


def _silu(x):
    return x * _sigmoid(x)


def _mm_kernel(*refs, n_w, n_e, n_o, nk, epilogue, a_prologue):
    a_ref = refs[0]
    w_refs = refs[1:1 + n_w]
    e_refs = refs[1 + n_w:1 + n_w + n_e]
    o_refs = refs[1 + n_w + n_e:1 + n_w + n_e + n_o]
    acc_refs = refs[1 + n_w + n_e + n_o:]
    a = a_ref[...]
    if a_prologue is not None:
        a = a_prologue(a)
    a = a.astype(BF16)
    prods = [jnp.dot(a, w[...].astype(BF16), preferred_element_type=F32) for w in w_refs]

    def finish(accs):
        outs = epilogue(accs, [e[...] for e in e_refs])
        for o, val in zip(o_refs, outs):
            o[...] = val.astype(o.dtype)

    if nk == 1:
        finish(prods)
        return
    k = pl.program_id(2)

    @pl.when(k == 0)
    def _():
        for acc, p in zip(acc_refs, prods):
            acc[...] = p

    @pl.when(k > 0)
    def _():
        for acc, p in zip(acc_refs, prods):
            acc[...] += p

    @pl.when(k == nk - 1)
    def _():
        finish([acc[...] for acc in acc_refs])


def _fused_matmul(a, a_block, a_map, ws, extras, outs, grid, tm, tn, epilogue, a_prologue=None):
    nk = grid[2]
    m_rows = grid[0] * tm
    in_specs = [pl.BlockSpec(a_block, a_map)]
    args = [a]
    for arr, blk, mp in list(ws) + list(extras):
        in_specs.append(pl.BlockSpec(blk, mp))
        args.append(arr)
    out_specs = [pl.BlockSpec((tm, tn), lambda i, j, k: (i, j)) for _ in outs]
    out_shape = [jax.ShapeDtypeStruct((m_rows, n), dt) for n, dt in outs]
    scratch = [pltpu.VMEM((tm, tn), F32) for _ in ws] if nk > 1 else []
    kern = functools.partial(_mm_kernel, n_w=len(ws), n_e=len(extras), n_o=len(outs), nk=nk,
                             epilogue=epilogue, a_prologue=a_prologue)
    res = pl.pallas_call(
        kern, grid=grid, in_specs=in_specs, out_specs=out_specs, out_shape=out_shape,
        scratch_shapes=scratch,
        compiler_params=_params(("parallel", "parallel", "arbitrary")),
    )(*args)
    return res


def _ada_kernel(c_ref, w_ref, b_ref, o_ref):
    c = _silu(c_ref[...])
    c_hi = c.astype(BF16)
    c_lo = (c - c_hi.astype(F32)).astype(BF16)
    w = w_ref[...]
    w_hi = w.astype(BF16)
    w_lo = (w - w_hi.astype(F32)).astype(BF16)
    acc = jnp.dot(c_hi, w_hi, preferred_element_type=F32)
    acc += jnp.dot(c_lo, w_hi, preferred_element_type=F32)
    acc += jnp.dot(c_hi, w_lo, preferred_element_type=F32)
    o_ref[...] = acc + b_ref[...]


def _adaln_all(cond, w_ada, b_ada):
    depth, d, n = w_ada.shape
    r = cond.shape[0]
    tn = _pick(n, (512, 256, 128))
    return pl.pallas_call(
        _ada_kernel,
        grid=(depth, n // tn),
        in_specs=[pl.BlockSpec((r, d), lambda l, j: (0, 0)),
                  pl.BlockSpec((None, d, tn), lambda l, j: (l, 0, j)),
                  pl.BlockSpec((None, 1, tn), lambda l, j: (l, 0, j))],
        out_specs=pl.BlockSpec((None, r, tn), lambda l, j: (l, 0, j)),
        out_shape=jax.ShapeDtypeStruct((depth, r, n), F32),
        compiler_params=_params(("parallel", "parallel")),
    )(cond, w_ada, b_ada.reshape(depth, 1, n))


def _norm_kernel(x_ref, g_ref, *rest, modulate):
    if modulate:
        sh_ref, sc_ref, o_ref = rest
    else:
        (o_ref,) = rest
    x = x_ref[...]
    ms = jnp.mean(x * x, axis=-1, keepdims=True)
    y = x * lax.rsqrt(ms + RMS_EPS) * g_ref[...]
    if modulate:
        y = y * (1.0 + sc_ref[...]) + sh_ref[...]
    o_ref[...] = y.astype(o_ref.dtype)


def _norm_mod(x, g, g_idx, mod, mod_idx, out_dtype, tr):
    m, d = x.shape
    in_specs = [pl.BlockSpec((tr, d), lambda i: (i, 0)),
                pl.BlockSpec((None, 1, d), lambda i: (g_idx, 0, 0))]
    args = [x, g]
    if mod is not None:
        in_specs += [pl.BlockSpec((None, 1, d), lambda i: (mod_idx(i, 0), 0, 0)),
                     pl.BlockSpec((None, 1, d), lambda i: (mod_idx(i, 1), 0, 0))]
        args += [mod, mod]
    return pl.pallas_call(
        functools.partial(_norm_kernel, modulate=mod is not None),
        grid=(m // tr,), in_specs=in_specs,
        out_specs=pl.BlockSpec((tr, d), lambda i: (i, 0)),
        out_shape=jax.ShapeDtypeStruct((m, d), out_dtype),
        compiler_params=_params(("parallel",)),
    )(*args)


def _conv_a_kernel(g_ref, w_ref, lg_ref, lb_ref, o_ref, pad_ref, *, n_prompt_tiles, seq, taps, halo):
    rows, ch = g_ref.shape
    half = taps // 2
    seg = jnp.where(pl.program_id(0) < n_prompt_tiles, seq, GRID_W)
    zeros = jnp.zeros((halo, ch), F32)
    pad_ref[pl.ds(0, halo), :] = zeros
    pad_ref[pl.ds(halo + rows, halo), :] = zeros
    pad_ref[pl.ds(halo, rows), :] = g_ref[...]
    pos = lax.broadcasted_iota(jnp.int32, (rows, 1), 0) & (seg - 1)
    acc = jnp.zeros((rows, ch), F32)
    for j in range(taps):
        off = j - half
        valid = jnp.logical_and(pos + off >= 0, pos + off < seg)
        xs = pad_ref[pl.ds(halo + off, rows), :]
        acc = acc + jnp.where(valid, xs, 0.0) * w_ref[pl.ds(j, 1), :]
    mu = jnp.mean(acc, axis=-1, keepdims=True)
    cen = acc - mu
    var = jnp.mean(cen * cen, axis=-1, keepdims=True)
    y = cen * lax.rsqrt(var + LN_EPS) * lg_ref[...] + lb_ref[...]
    o_ref[...] = _silu(y).astype(o_ref.dtype)


def _conv_a(g, w_conv, ln_g, ln_b, l, n_prompt_rows, seq, tr):
    m, ch = g.shape
    taps = w_conv.shape[1]
    halo = 2 * SUBLANES
    assert taps // 2 < halo and tr % seq == 0 and tr % GRID_W == 0
    return pl.pallas_call(
        functools.partial(_conv_a_kernel, n_prompt_tiles=n_prompt_rows // tr, seq=seq, taps=taps, halo=halo),
        grid=(m // tr,),
        in_specs=[pl.BlockSpec((tr, ch), lambda i: (i, 0)),
                  pl.BlockSpec((None, taps, ch), lambda i: (l, 0, 0)),
                  pl.BlockSpec((None, 1, ch), lambda i: (l, 0, 0)),
                  pl.BlockSpec((None, 1, ch), lambda i: (l, 0, 0))],
        out_specs=pl.BlockSpec((tr, ch), lambda i: (i, 0)),
        out_shape=jax.ShapeDtypeStruct((m, ch), BF16),
        scratch_shapes=[pltpu.VMEM((tr + 2 * halo, ch), F32)],
        compiler_params=_params(("parallel",)),
    )(g, w_conv, ln_g.reshape(-1, 1, ch), ln_b.reshape(-1, 1, ch))


def _shift_rows(x, lo_row, hi_row):
    rows = x.shape[0]
    r = lax.broadcasted_iota(jnp.int32, (rows, 1), 0)
    prev = jnp.where(r == 0, lo_row, pltpu.roll(x, 1, axis=0))
    nxt = jnp.where(r == rows - 1, hi_row, pltpu.roll(x, rows - 1, axis=0))
    return prev, nxt


def _conv_b_kernel(p_ref, lo_ref, hi_ref, bg_ref, w_ref, o_ref, *, n_prompt_tiles, tiles_per_img):
    rows, ch = p_ref.shape
    i = pl.program_id(0)
    p = p_ref[...]
    w0 = w_ref[pl.ds(0, 1), :]
    w1 = w_ref[pl.ds(1, 1), :]
    w2 = w_ref[pl.ds(2, 1), :]

    @pl.when(i < n_prompt_tiles)
    def _():
        zero = jnp.zeros((1, ch), F32)
        prev, nxt = _shift_rows(p, zero, zero)
        o_ref[...] = (bg_ref[...] * (w0 * prev + w1 * p + w2 * nxt)).astype(o_ref.dtype)

    @pl.when(i >= n_prompt_tiles)
    def _():
        t = (i - n_prompt_tiles) % tiles_per_img
        m_lo = jnp.where(t == 0, 0.0, 1.0)
        m_hi = jnp.where(t == tiles_per_img - 1, 0.0, 1.0)
        prev = jnp.concatenate([lo_ref[...] * m_lo, p[:rows - GRID_W]], axis=0)
        nxt = jnp.concatenate([p[GRID_W:], hi_ref[...] * m_hi], axis=0)
        o_ref[...] = (bg_ref[...] * (w0 * prev + w1 * p + w2 * nxt)).astype(o_ref.dtype)


def _conv_b(p, bg, w_conv, l, n_prompt_rows, seq, dec_seq, tr, tc):
    m, ch = p.shape
    assert tr == seq and tr % GRID_W == 0 and dec_seq % tr == 0
    hb = tr // GRID_W
    n_hblocks = m // GRID_W
    return pl.pallas_call(
        functools.partial(_conv_b_kernel, n_prompt_tiles=n_prompt_rows // tr, tiles_per_img=dec_seq // tr),
        grid=(m // tr, ch // tc),
        in_specs=[pl.BlockSpec((tr, tc), lambda i, j: (i, j)),
                  pl.BlockSpec((GRID_W, tc), lambda i, j: (jnp.maximum(i * hb - 1, 0), j)),
                  pl.BlockSpec((GRID_W, tc), lambda i, j: (jnp.minimum((i + 1) * hb, n_hblocks - 1), j)),
                  pl.BlockSpec((tr, tc), lambda i, j: (i, j)),
                  pl.BlockSpec((None, 3, tc), lambda i, j: (l, 0, j))],
        out_specs=pl.BlockSpec((tr, tc), lambda i, j: (i, j)),
        out_shape=jax.ShapeDtypeStruct((m, ch), BF16),
        compiler_params=_params(("parallel", "parallel")),
    )(p, p, p, bg, w_conv)


def _shift_conv_kernel(x_ref, lo_ref, hi_ref, w_ref, o_ref, *, n_prompt_tiles, tiles_per_seq):
    i = pl.program_id(0)
    t = (i - n_prompt_tiles) % tiles_per_seq
    is_prompt = i < n_prompt_tiles
    m_lo = jnp.where(jnp.logical_or(is_prompt, t == 0), 0.0, 1.0)
    m_hi = jnp.where(jnp.logical_or(is_prompt, t == tiles_per_seq - 1), 0.0, 1.0)
    x = x_ref[...]
    prev, nxt = _shift_rows(x, lo_ref[pl.ds(SUBLANES - 1, 1), :] * m_lo, hi_ref[pl.ds(0, 1), :] * m_hi)
    o_ref[...] = (w_ref[pl.ds(0, 1), :] * prev + w_ref[pl.ds(1, 1), :] * x
                  + w_ref[pl.ds(2, 1), :] * nxt).astype(o_ref.dtype)


def _shift_conv(x, w_conv, l, n_prompt_rows, seq, dec_seq, tr, tc):
    m, ch = x.shape
    assert tr == seq and dec_seq % tr == 0
    hb = tr // SUBLANES
    n_hblocks = m // SUBLANES
    return pl.pallas_call(
        functools.partial(_shift_conv_kernel, n_prompt_tiles=n_prompt_rows // tr, tiles_per_seq=dec_seq // tr),
        grid=(m // tr, ch // tc),
        in_specs=[pl.BlockSpec((tr, tc), lambda i, j: (i, j)),
                  pl.BlockSpec((SUBLANES, tc), lambda i, j: (jnp.maximum(i * hb - 1, 0), j)),
                  pl.BlockSpec((SUBLANES, tc), lambda i, j: (jnp.minimum((i + 1) * hb, n_hblocks - 1), j)),
                  pl.BlockSpec((None, 3, tc), lambda i, j: (l, 0, j))],
        out_specs=pl.BlockSpec((tr, tc), lambda i, j: (i, j)),
        out_shape=jax.ShapeDtypeStruct((m, ch), F32),
        compiler_params=_params(("parallel", "parallel")),
    )(x, x, x, w_conv)


def _scan_kernel(r_ref, k_ref, v_ref, w_ref, a_ref, kk_ref, ka_ref, rk_ref, s0_ref,
                 y_ref, c_ref, sf_ref, s_ref, sa_ref, *, steps, n_chunks):
    d = pl.program_id(0)
    c = pl.program_id(2)

    @pl.when(c == 0)
    def _():
        s_ref[...] = s0_ref[...]

    k_k = kk_ref[...]
    k_a = ka_ref[...]
    r_k = rk_ref[...]

    def step(n, carry):
        tt = jnp.where(d == 0, n, steps - 1 - n)
        r_t = r_ref[tt]
        k_t = k_ref[tt]
        w_t = w_ref[tt]
        a_t = a_ref[tt]
        kk = k_t * k_k
        nrm = jnp.sqrt(jnp.sum(kk * kk, axis=0, keepdims=True))
        kk = kk / jnp.maximum(nrm, 1e-12)
        b_t = kk * a_t
        kd = k_t * (1.0 + (a_t - 1.0) * k_a)
        c_ref[tt] = jnp.sum(r_t * kd * r_k, axis=0, keepdims=True)

        def sa_pass(vi, carry2):
            sa_ref[pl.ds(vi, 1), :] = jnp.sum(s_ref[vi] * kk, axis=0, keepdims=True)
            return carry2

        lax.fori_loop(0, HEAD, sa_pass, 0, unroll=8)

        def upd_pass(vi, carry2):
            sa_v = sa_ref[pl.ds(vi, 1), :]
            v_v = v_ref[tt, pl.ds(vi, 1), :]
            sn = s_ref[vi] * w_t - sa_v * b_t + v_v * kd
            s_ref[vi] = sn
            y_ref[tt, pl.ds(vi, 1), :] = jnp.sum(sn * r_t, axis=0, keepdims=True)
            return carry2

        lax.fori_loop(0, HEAD, upd_pass, 0, unroll=8)
        return carry

    lax.fori_loop(0, steps, step, 0)

    @pl.when(c == n_chunks - 1)
    def _():
        sf_ref[...] = s_ref[...]


def _wkv_scan(r, k, v, w, a, k_k, k_a, r_k, s0, steps):
    seq, _, nh = r.shape
    n_chunks = seq // steps
    grid = (2, nh // LANES, n_chunks)

    def tmap(d, c):
        return jnp.where(d == 0, c, n_chunks - 1 - c)

    shared = pl.BlockSpec((steps, HEAD, LANES), lambda d, g, c: (tmap(d, c), 0, g))
    per_dir = pl.BlockSpec((None, steps, HEAD, LANES), lambda d, g, c: (d, tmap(d, c), 0, g))
    par = pl.BlockSpec((HEAD, LANES), lambda d, g, c: (0, g))
    st = pl.BlockSpec((None, HEAD, HEAD, LANES), lambda d, g, c: (d, 0, 0, g))
    return pl.pallas_call(
        functools.partial(_scan_kernel, steps=steps, n_chunks=n_chunks),
        grid=grid,
        in_specs=[shared, shared, shared, per_dir, per_dir, par, par, par, st],
        out_specs=[per_dir,
                   pl.BlockSpec((None, steps, 1, LANES), lambda d, g, c: (d, tmap(d, c), 0, g)),
                   st],
        out_shape=[jax.ShapeDtypeStruct((2, seq, HEAD, nh), F32),
                   jax.ShapeDtypeStruct((2, seq, 1, nh), F32),
                   jax.ShapeDtypeStruct((2, HEAD, HEAD, nh), F32)],
        scratch_shapes=[pltpu.VMEM((HEAD, HEAD, LANES), F32), pltpu.VMEM((HEAD, LANES), F32)],
        compiler_params=_params(("parallel", "parallel", "arbitrary")),
    )(r, k, v, w, a, k_k, k_a, r_k, s0)


def _scan_post_kernel(y_ref, c_ref, v_ref, g_ref, gg_ref, gb_ref, o_ref):
    y = y_ref[0] + y_ref[1]
    mu = jnp.mean(y, axis=1, keepdims=True)
    cen = y - mu
    var = jnp.mean(cen * cen, axis=1, keepdims=True)
    yn = cen * lax.rsqrt(var + GN_EPS) * gg_ref[...][None] + gb_ref[...][None]
    yn = yn + (c_ref[0] + c_ref[1]) * v_ref[...]
    o_ref[...] = (yn * g_ref[...]).astype(o_ref.dtype)


def _scan_post(y, cb, v, gate, gn_g, gn_b, steps):
    _, seq, _, nh = y.shape
    tile = pl.BlockSpec((steps, HEAD, LANES), lambda g, c: (c, 0, g))
    par = pl.BlockSpec((HEAD, LANES), lambda g, c: (0, g))
    return pl.pallas_call(
        _scan_post_kernel,
        grid=(nh // LANES, seq // steps),
        in_specs=[pl.BlockSpec((2, steps, HEAD, LANES), lambda g, c: (0, c, 0, g)),
                  pl.BlockSpec((2, steps, 1, LANES), lambda g, c: (0, c, 0, g)),
                  tile, tile, par, par],
        out_specs=tile,
        out_shape=jax.ShapeDtypeStruct((seq, HEAD, nh), BF16),
        compiler_params=_params(("parallel", "parallel")),
    )(y, cb, v, gate, gn_g, gn_b)


def _to_heads(x, b, seq):
    h = x.shape[-1] // HEAD
    return x.reshape(b, seq, h, HEAD).transpose(1, 3, 0, 2).reshape(seq, HEAD, b * h)


def _from_heads(x, b):
    seq, _, nh = x.shape
    h = nh // b
    return x.reshape(seq, HEAD, b, h).transpose(2, 0, 3, 1).reshape(b * seq, h * HEAD)


def _param_heads(p, b):
    h = p.size // HEAD
    return jnp.tile(p.reshape(h, HEAD).T, (1, b))


def kernel(x_prompt, x_sample, state_rwkv, c, c_ctx, w_ada, b_ada, norm_g, w_ffn_gate, w_ffn_up,
           w_ffn_down, w_in, w_conv_a, ln_a_g, ln_a_b, w_out_a, w_conv_b, w_out_b, w_shift, k_k,
           k_a, r_k, w0, w_w2, a0, w_a2, w_g2, gn_g, gn_b, w_out_c, w_o, final_g):
    batch, seq, d = x_prompt.shape
    dec_batch, dec_seq, _ = x_sample.shape
    depth = w_ada.shape[0]
    d_ff = w_ffn_gate.shape[-1]
    w_a = w_out_a.shape[1]
    w_b = w_out_b.shape[1]
    w_c = w_out_c.shape[1]
    n_heads = w_c // HEAD
    lora_w = w_w2.shape[2]
    lora_a = w_a2.shape[2]
    lora_g = w_g2.shape[1]
    n_prompt = batch * seq
    m = n_prompt + dec_batch * dec_seq
    assert seq & (seq - 1) == 0 and GRID_W & (GRID_W - 1) == 0 and dec_seq % GRID_W == 0
    assert lora_w == LANES and lora_a == LANES

    off_a = 0
    off_b = off_a + 2 * w_a
    off_rkv = off_b + 3 * w_b
    off_lw = off_rkv + 3 * w_c
    off_la = off_lw + 2 * lora_w
    off_lg = off_la + 2 * lora_a
    off_g = off_lg + lora_g
    lg_pad = -(-lora_g // LANES) * LANES

    def grp(i, rows):
        r0 = i * rows
        return jnp.where(r0 < n_prompt, 0, 1 + (r0 - n_prompt) // dec_seq)

    n_cond = 1 + dec_batch
    rpad = -(-n_cond // SUBLANES) * SUBLANES
    cond = jnp.concatenate([c_ctx[None, :], c, jnp.zeros((rpad - n_cond, d), F32)], axis=0)
    mod = _adaln_all(cond, w_ada, b_ada).reshape(depth * rpad * N_MOD, 1, d)

    def mod_row(l, g, idx):
        return (l * rpad + g) * N_MOD + idx

    norm_g3 = norm_g.reshape(depth * 3, 1, d)
    x = jnp.concatenate([x_prompt.reshape(n_prompt, d), x_sample.reshape(m - n_prompt, d)], axis=0)

    tr = seq
    tm_mm = _pick(math_gcd(n_prompt, dec_seq), (1024, 512, 256, 128))

    def mod_block(l, idx, rows, tn):
        return (mod, (None, 1, tn), lambda i, j, k: (mod_row(l, grp(i, rows), idx), 0, j))

    def ffn_half(x, l, half):
        base = 0 if half == 0 else 6
        h = _norm_mod(x, norm_g3, l * 3 + (0 if half == 0 else 2), mod,
                      lambda i, which: mod_row(l, grp(i, tr), base + which), BF16, tr)
        tn = _pick(d_ff, (256, 128))
        wg = (w_ffn_gate, (None, None, d, tn), lambda i, j, k: (l, half, 0, j))
        wu = (w_ffn_up, (None, None, d, tn), lambda i, j, k: (l, half, 0, j))
        (hg,) = _fused_matmul(
            h, (tm_mm, d), lambda i, j, k: (i, 0), [wg, wu], [], [(d_ff, BF16)],
            (m // tm_mm, d_ff // tn, 1), tm_mm, tn,
            lambda accs, ex: [_silu(accs[0]) * accs[1]])
        tk = _pick(d_ff, (512, 256, 128))
        tn2 = _pick(d, (1024, 512, 256, 128))
        wd = (w_ffn_down, (None, None, tk, tn2), lambda i, j, k: (l, half, k, j))
        xt = (x, (tm_mm, tn2), lambda i, j, k: (i, j))
        (x_new,) = _fused_matmul(
            hg, (tm_mm, tk), lambda i, j, k: (i, k), [wd], [xt, mod_block(l, base + 2, tm_mm, tn2)],
            [(d, F32)], (m // tm_mm, d // tn2, d_ff // tk), tm_mm, tn2,
            lambda accs, ex: [ex[0] + 0.5 * ex[1] * accs[0]])
        return x_new

    def in_proj(h, l, col0, n_cols, n_w, tn, epilogue, outs, tm_):
        ws = [(w_in, (None, d, tn), functools.partial(
            lambda i, j, k, q: (l, 0, (col0 + q * n_cols) // tn + j), q=q)) for q in range(n_w)]
        assert all((col0 + q * n_cols) % tn == 0 for q in range(n_w))
        return _fused_matmul(h, (tm_, d), lambda i, j, k: (i, 0), ws, [], outs,
                             (m // tm_, n_cols // tn, 1), tm_, tn, epilogue)

    new_states = []
    s_zero = jnp.zeros((2, HEAD, HEAD, batch * n_heads), F32)
    for l in range(depth):
        x = ffn_half(x, l, 0)

        h = _norm_mod(x, norm_g3, l * 3 + 1, mod, lambda i, which: mod_row(l, grp(i, tr), 3 + which), BF16, tr)
        tn_s = _pick(w_a, (256, 128))
        (glu,) = in_proj(h, l, off_a, w_a, 2, tn_s, lambda accs, ex: [accs[0] * _sigmoid(accs[1])],
                         [(w_a, F32)], tm_mm)
        tm_b = _pick(tm_mm, (512, 256, 128))
        p_b, bg = in_proj(h, l, off_b, w_b, 3, tn_s, lambda accs, ex: [accs[2] * accs[0], accs[1]],
                          [(w_b, F32), (w_b, F32)], tm_b)
        tn_l = _pick(w_c, (512, 256, 128))
        (zrkv,) = in_proj(h, l, off_rkv, 3 * w_c, 1, tn_l, lambda accs, ex: [accs[0]], [(3 * w_c, F32)], tm_mm)
        (lw,) = in_proj(h, l, off_lw, 2 * lora_w, 1, 2 * lora_w, lambda accs, ex: [jnp.tanh(accs[0])],
                        [(2 * lora_w, F32)], tm_mm)
        (la,) = in_proj(h, l, off_la, 2 * lora_a, 1, 2 * lora_a, lambda accs, ex: [accs[0]],
                        [(2 * lora_a, F32)], tm_mm)
        (lg,) = in_proj(h, l, off_lg, lg_pad, 1, lg_pad, lambda accs, ex: [_sigmoid(accs[0])],
                        [(lg_pad, BF16)], tm_mm)
        w_gates = lax.slice_in_dim(w_in[l], off_g, off_g + 3 * d, axis=1)
        wgt = (w_gates, (d, tn_l), lambda i, j, k: (0, j))
        (sig_g,) = _fused_matmul(h, (tm_mm, d), lambda i, j, k: (i, 0), [wgt], [], [(3 * d, F32)],
                                 (m // tm_mm, 3 * d // tn_l, 1), tm_mm, tn_l,
                                 lambda accs, ex: [_sigmoid(accs[0])])

        feat_a = _conv_a(glu, w_conv_a, ln_a_g, ln_a_b, l, n_prompt, seq, tr)
        feat_b = _conv_b(p_b, bg, w_conv_b, l, n_prompt, seq, dec_seq, tr, _pick(w_b, (1024, 512, 256, 128)))

        rkv = _shift_conv(zrkv, w_shift, l, n_prompt, seq, dec_seq, tr, _pick(3 * w_c, (1024, 512, 256, 128)))
        tn_c = _pick(w_c, (1024, 512, 256, 128))

        def lora(src, w2, bias, dd, epi):
            wspec = (w2, (None, None, LANES, tn_c), lambda i, j, k: (l, dd, 0, j))
            bspec = (bias.reshape(depth * 2, 1, w_c), (None, 1, tn_c), lambda i, j, k: (l * 2 + dd, 0, j))
            (o,) = _fused_matmul(src, (tm_mm, LANES), lambda i, j, k: (i, dd), [wspec], [bspec], [(w_c, F32)],
                                 (m // tm_mm, w_c // tn_c, 1), tm_mm, tn_c, epi)
            return o

        decay = [lora(lw, w_w2, w0, dd, lambda accs, ex: [jnp.exp(-DECAY_SCALE * _sigmoid(ex[0] + accs[0]))])
                 for dd in range(2)]
        lr = [lora(la, w_a2, a0, dd, lambda accs, ex: [_sigmoid(ex[0] + accs[0])]) for dd in range(2)]
        w_g2p = jnp.pad(w_g2[l], ((0, lg_pad - lora_g), (0, 0)))
        (gate_c,) = _fused_matmul(lg, (tm_mm, lg_pad), lambda i, j, k: (i, 0),
                                  [(w_g2p, (lg_pad, tn_c), lambda i, j, k: (0, j))], [], [(w_c, F32)],
                                  (m // tm_mm, w_c // tn_c, 1), tm_mm, tn_c, lambda accs, ex: [accs[0]])

        yc_parts = []
        for (row0, nb, ln, s0) in ((0, batch, seq, s_zero),
                                   (n_prompt, dec_batch, dec_seq,
                                    state_rwkv[:, l].transpose(1, 3, 4, 0, 2).reshape(2, HEAD, HEAD, dec_batch * n_heads))):
            rows = slice(row0, row0 + nb * ln)
            r_h = _to_heads(rkv[rows, 0:w_c], nb, ln)
            k_h = _to_heads(rkv[rows, w_c:2 * w_c], nb, ln)
            v_h = _to_heads(rkv[rows, 2 * w_c:3 * w_c], nb, ln)
            w_h = jnp.stack([_to_heads(decay[dd][rows], nb, ln) for dd in range(2)])
            a_h = jnp.stack([_to_heads(lr[dd][rows], nb, ln) for dd in range(2)])
            g_h = _to_heads(gate_c[rows], nb, ln)
            steps = _pick(ln, (64, 32, 16, 8))
            y_h, cb, s_fin = _wkv_scan(r_h, k_h, v_h, w_h, a_h, _param_heads(k_k[l], nb), _param_heads(k_a[l], nb),
                                       _param_heads(r_k[l], nb), s0, steps)
            yc_h = _scan_post(y_h, cb, v_h, g_h, _param_heads(gn_g[l], nb), _param_heads(gn_b[l], nb), steps)
            yc_parts.append(_from_heads(yc_h, nb))
            if row0 == 0:
                new_states.append(s_fin)
        feat_c = jnp.concatenate(yc_parts, axis=0)

        tn_m = _pick(d, (512, 256, 128))
        merged = None
        for q, (feat, w_out, kdim) in enumerate(((feat_a, w_out_a, w_a), (feat_b, w_out_b, w_b), (feat_c, w_out_c, w_c))):
            tk = _pick(kdim, (1024, 512, 256, 128))
            wspec = (w_out, (None, tk, tn_m), lambda i, j, k: (l, k, j))
            gspec = (sig_g, (tm_mm, tn_m), functools.partial(lambda i, j, k, q: (i, q * (d // tn_m) + j), q=q))
            extras = [gspec] if merged is None else [gspec, (merged, (tm_mm, tn_m), lambda i, j, k: (i, j))]
            last = q == 2
            (merged,) = _fused_matmul(
                feat, (tm_mm, tk), lambda i, j, k: (i, k), [wspec], extras, [(d, BF16 if last else F32)],
                (m // tm_mm, d // tn_m, kdim // tk), tm_mm, tn_m,
                (lambda accs, ex: [ex[0] * accs[0]]) if q == 0 else (lambda accs, ex: [ex[1] + ex[0] * accs[0]]))
        tk = _pick(d, (1024, 512, 256, 128))
        (x,) = _fused_matmul(
            merged, (tm_mm, tk), lambda i, j, k: (i, k), [(w_o, (None, tk, tn_m), lambda i, j, k: (l, k, j))],
            [(x, (tm_mm, tn_m), lambda i, j, k: (i, j)), mod_block(l, 5, tm_mm, tn_m)], [(d, F32)],
            (m // tm_mm, d // tn_m, d // tk), tm_mm, tn_m, lambda accs, ex: [ex[0] + ex[1] * accs[0]])

        x = ffn_half(x, l, 1)

    y = _norm_mod(x, final_g.reshape(1, 1, d), 0, None, None, F32, tr)
    y_prompt = y[:n_prompt].reshape(batch, seq, d)
    y_sample = y[n_prompt:].reshape(dec_batch, dec_seq, d)
    st = jnp.stack(new_states)
    st = st.reshape(depth, 2, HEAD, HEAD, batch, n_heads).transpose(4, 0, 1, 5, 2, 3)
    return (y_prompt, y_sample, st)


def math_gcd(a, b):
    while b:
        a, b = b, a % b
    return a
```

```python
import functools
import math

import jax
import jax.numpy as jnp
from jax import lax
from jax.experimental import pallas as pl
from jax.experimental.pallas import tpu as pltpu

F32 = jnp.float32
BF16 = jnp.bfloat16

GRID_W = 64
HEAD = 64
N_MOD = 9
RMS_EPS = 1e-6
LN_EPS = 1e-5
GN_EPS = HEAD * 1e-5
DECAY_SCALE = 0.6065306597126334
LANES = 128
SUBLANES = 8
VMEM_LIMIT = 56 * 1024 * 1024
ROWS_PER_ITER = 8


def _pick(n, prefs):
    for p in prefs:
        if n % p == 0:
            return p
    raise ValueError(f"no tile in {prefs} divides {n}")


def _split_k(k, prefs):
    for tk in prefs:
        tail = k % tk
        if k >= tk and (tail == 0 or (tail % LANES == 0 and (k - tail) % tail == 0)):
            return tk, tail
    raise ValueError(f"no K tiling for {k}")


def _params(sem):
    return pltpu.CompilerParams(dimension_semantics=sem, vmem_limit_bytes=VMEM_LIMIT)


def _sigmoid(x):
    return 1.0 / (1.0 + jnp.exp(-x))


def _silu(x):
    return x * _sigmoid(x)


def _mm_kernel(*refs, n_w, n_t, n_e, n_o, nk, epilogue, a_prologue):
    a_ref = refs[0]
    w_refs = refs[1:1 + n_w]
    pos = 1 + n_w
    t_refs = refs[pos:pos + 2 * n_t]
    pos += 2 * n_t
    e_refs = refs[pos:pos + n_e]
    pos += n_e
    o_refs = refs[pos:pos + n_o]
    acc_refs = refs[pos + n_o:]
    a = a_ref[...]
    if a_prologue is not None:
        a = a_prologue(a)
    a = a.astype(BF16)
    prods = [jnp.dot(a, w[...].astype(BF16), preferred_element_type=F32) for w in w_refs]

    def first(ps):
        if n_t:
            tail = jnp.dot(t_refs[0][...].astype(BF16), t_refs[1][...].astype(BF16), preferred_element_type=F32)
            return [ps[0] + tail] + ps[1:]
        return ps

    def finish(accs):
        outs = epilogue(accs, [e[...] for e in e_refs])
        for o, val in zip(o_refs, outs):
            o[...] = val.astype(o.dtype)

    if nk == 1:
        finish(first(prods))
        return
    k = pl.program_id(2)

    @pl.when(k == 0)
    def _():
        for acc, p in zip(acc_refs, first(prods)):
            acc[...] = p

    @pl.when(k > 0)
    def _():
        for acc, p in zip(acc_refs, prods):
            acc[...] += p

    @pl.when(k == nk - 1)
    def _():
        finish([acc[...] for acc in acc_refs])


def _fused_matmul(name, a, a_block, a_map, ws, extras, outs, grid, tm, tn, epilogue, a_prologue=None, tail=None):
    nk = grid[2]
    m_rows = grid[0] * tm
    in_specs = [pl.BlockSpec(a_block, a_map)]
    args = [a]
    operands = list(ws)
    if tail is not None:
        operands += [tail[0:3], tail[3:6]]
    for arr, blk, mp in operands + list(extras):
        in_specs.append(pl.BlockSpec(blk, mp))
        args.append(arr)
    out_specs = [pl.BlockSpec((tm, tn), lambda i, j, k: (i, j)) for _ in outs]
    out_shape = [jax.ShapeDtypeStruct((m_rows, n), dt) for n, dt in outs]
    scratch = [pltpu.VMEM((tm, tn), F32) for _ in ws] if nk > 1 else []
    kern = functools.partial(_mm_kernel, n_w=len(ws), n_t=0 if tail is None else 1, n_e=len(extras),
                             n_o=len(outs), nk=nk, epilogue=epilogue, a_prologue=a_prologue)
    return pl.pallas_call(
        kern, grid=grid, in_specs=in_specs, out_specs=out_specs, out_shape=out_shape,
        scratch_shapes=scratch, name=name,
        compiler_params=_params(("parallel", "parallel", "arbitrary")),
    )(*args)


def _merge_kernel(a_ref, w_ref, g0_ref, g1_ref, g2_ref, o_ref, acc0, acc1, acc2, *, bounds):
    k = pl.program_id(2)
    p = jnp.dot(a_ref[...], w_ref[...].astype(BF16), preferred_element_type=F32)
    lo = 0
    for acc, hi in zip((acc0, acc1, acc2), bounds):
        @pl.when(k == lo)
        def _(acc=acc):
            acc[...] = p

        @pl.when(jnp.logical_and(k > lo, k < hi))
        def _(acc=acc):
            acc[...] += p
        lo = hi

    @pl.when(k == bounds[-1] - 1)
    def _():
        o_ref[...] = (g0_ref[...].astype(F32) * acc0[...] + g1_ref[...].astype(F32) * acc1[...]
                      + g2_ref[...].astype(F32) * acc2[...]).astype(o_ref.dtype)


def _merge(feat, w_cat, sig_g, widths, d, tm, tn, tk):
    m, k_tot = feat.shape
    bounds, acc = [], 0
    for wd in widths:
        assert wd % tk == 0
        acc += wd // tk
        bounds.append(acc)
    nj = d // tn
    gates = [pl.BlockSpec((tm, tn), functools.partial(lambda i, j, k, q: (i, q * nj + j), q=q)) for q in range(3)]
    return pl.pallas_call(
        functools.partial(_merge_kernel, bounds=tuple(bounds)),
        grid=(m // tm, nj, k_tot // tk),
        in_specs=[pl.BlockSpec((tm, tk), lambda i, j, k: (i, k)),
                  pl.BlockSpec((tk, tn), lambda i, j, k: (k, j))] + gates,
        out_specs=pl.BlockSpec((tm, tn), lambda i, j, k: (i, j)),
        out_shape=jax.ShapeDtypeStruct((m, d), BF16),
        scratch_shapes=[pltpu.VMEM((tm, tn), F32)] * 3, name="merge",
        compiler_params=_params(("parallel", "parallel", "arbitrary")),
    )(feat, w_cat, sig_g, sig_g, sig_g)


def _ada_kernel(c_ref, w_ref, b_ref, o_ref):
    c = _silu(c_ref[...])
    c_hi = c.astype(BF16)
    c_lo = (c - c_hi.astype(F32)).astype(BF16)
    w = w_ref[...]
    w_hi = w.astype(BF16)
    w_lo = (w - w_hi.astype(F32)).astype(BF16)
    acc = jnp.dot(c_hi, w_hi, preferred_element_type=F32)
    acc += jnp.dot(c_lo, w_hi, preferred_element_type=F32)
    acc += jnp.dot(c_hi, w_lo, preferred_element_type=F32)
    o_ref[...] = acc + b_ref[...]


def _adaln_all(cond, w_ada, b_ada):
    depth, d, n = w_ada.shape
    r = cond.shape[0]
    tn = _pick(n, (512, 256, 128))
    return pl.pallas_call(
        _ada_kernel,
        grid=(depth, n // tn),
        in_specs=[pl.BlockSpec((r, d), lambda l, j: (0, 0)),
                  pl.BlockSpec((None, d, tn), lambda l, j: (l, 0, j)),
                  pl.BlockSpec((None, 1, tn), lambda l, j: (l, 0, j))],
        out_specs=pl.BlockSpec((None, r, tn), lambda l, j: (l, 0, j)),
        out_shape=jax.ShapeDtypeStruct((depth, r, n), F32), name="adaln",
        compiler_params=_params(("parallel", "parallel")),
    )(cond, w_ada, b_ada.reshape(depth, 1, n))


def _norm_kernel(x_ref, g_ref, *rest, modulate):
    if modulate:
        sh_ref, sc_ref, o_ref = rest
    else:
        (o_ref,) = rest
    x = x_ref[...]
    ms = jnp.mean(x * x, axis=-1, keepdims=True)
    y = x * lax.rsqrt(ms + RMS_EPS) * g_ref[...]
    if modulate:
        y = y * (1.0 + sc_ref[...]) + sh_ref[...]
    o_ref[...] = y.astype(o_ref.dtype)


def _norm_mod(x, g, g_idx, mod, mod_idx, out_dtype, tr):
    m, d = x.shape
    in_specs = [pl.BlockSpec((tr, d), lambda i: (i, 0)),
                pl.BlockSpec((None, 1, d), lambda i: (g_idx, 0, 0))]
    args = [x, g]
    if mod is not None:
        in_specs += [pl.BlockSpec((None, 1, d), lambda i: (mod_idx(i, 0), 0, 0)),
                     pl.BlockSpec((None, 1, d), lambda i: (mod_idx(i, 1), 0, 0))]
        args += [mod, mod]
    return pl.pallas_call(
        functools.partial(_norm_kernel, modulate=mod is not None),
        grid=(m // tr,), in_specs=in_specs,
        out_specs=pl.BlockSpec((tr, d), lambda i: (i, 0)),
        out_shape=jax.ShapeDtypeStruct((m, d), out_dtype), name="rmsnorm",
        compiler_params=_params(("parallel",)),
    )(*args)


def _conv_a_kernel(g_ref, w_ref, lg_ref, lb_ref, o_ref, pad_ref, *, n_prompt_tiles, seq, taps, halo):
    rows, ch = g_ref.shape
    half = taps // 2
    seg = jnp.where(pl.program_id(0) < n_prompt_tiles, seq, GRID_W)
    zeros = jnp.zeros((halo, ch), F32)
    pad_ref[pl.ds(0, halo), :] = zeros
    pad_ref[pl.ds(halo + rows, halo), :] = zeros
    pad_ref[pl.ds(halo, rows), :] = g_ref[...]
    pos = lax.broadcasted_iota(jnp.int32, (rows, 1), 0) & (seg - 1)
    acc = jnp.zeros((rows, ch), F32)
    for j in range(taps):
        off = j - half
        valid = jnp.logical_and(pos + off >= 0, pos + off < seg)
        xs = pad_ref[pl.ds(halo + off, rows), :]
        acc = acc + jnp.where(valid, xs, 0.0) * w_ref[pl.ds(j, 1), :]
    mu = jnp.mean(acc, axis=-1, keepdims=True)
    cen = acc - mu
    var = jnp.mean(cen * cen, axis=-1, keepdims=True)
    y = cen * lax.rsqrt(var + LN_EPS) * lg_ref[...] + lb_ref[...]
    o_ref[...] = _silu(y).astype(o_ref.dtype)


def _conv_a(g, w_conv, ln_g, ln_b, l, n_prompt_rows, seq, tr):
    m, ch = g.shape
    taps = w_conv.shape[1]
    halo = 2 * SUBLANES
    assert taps // 2 < halo and tr % seq == 0 and tr % GRID_W == 0
    return pl.pallas_call(
        functools.partial(_conv_a_kernel, n_prompt_tiles=n_prompt_rows // tr, seq=seq, taps=taps, halo=halo),
        grid=(m // tr,),
        in_specs=[pl.BlockSpec((tr, ch), lambda i: (i, 0)),
                  pl.BlockSpec((None, taps, ch), lambda i: (l, 0, 0)),
                  pl.BlockSpec((None, 1, ch), lambda i: (l, 0, 0)),
                  pl.BlockSpec((None, 1, ch), lambda i: (l, 0, 0))],
        out_specs=pl.BlockSpec((tr, ch), lambda i: (i, 0)),
        out_shape=jax.ShapeDtypeStruct((m, ch), BF16),
        scratch_shapes=[pltpu.VMEM((tr + 2 * halo, ch), F32)], name="conv_a",
        compiler_params=_params(("parallel",)),
    )(g, w_conv, ln_g.reshape(-1, 1, ch), ln_b.reshape(-1, 1, ch))


def _shift_rows(x, lo_row, hi_row):
    rows = x.shape[0]
    r = lax.broadcasted_iota(jnp.int32, (rows, 1), 0)
    prev = jnp.where(r == 0, lo_row, pltpu.roll(x, 1, axis=0))
    nxt = jnp.where(r == rows - 1, hi_row, pltpu.roll(x, rows - 1, axis=0))
    return prev, nxt


def _conv_b_kernel(p_ref, lo_ref, hi_ref, bg_ref, w_ref, o_ref, *, n_prompt_tiles, tiles_per_img):
    rows, ch = p_ref.shape
    i = pl.program_id(0)
    p = p_ref[...]
    w0 = w_ref[pl.ds(0, 1), :]
    w1 = w_ref[pl.ds(1, 1), :]
    w2 = w_ref[pl.ds(2, 1), :]

    @pl.when(i < n_prompt_tiles)
    def _():
        zero = jnp.zeros((1, ch), F32)
        prev, nxt = _shift_rows(p, zero, zero)
        o_ref[...] = (bg_ref[...] * (w0 * prev + w1 * p + w2 * nxt)).astype(o_ref.dtype)

    @pl.when(i >= n_prompt_tiles)
    def _():
        t = (i - n_prompt_tiles) % tiles_per_img
        m_lo = jnp.where(t == 0, 0.0, 1.0)
        m_hi = jnp.where(t == tiles_per_img - 1, 0.0, 1.0)
        prev = jnp.concatenate([lo_ref[...] * m_lo, p[:rows - GRID_W]], axis=0)
        nxt = jnp.concatenate([p[GRID_W:], hi_ref[...] * m_hi], axis=0)
        o_ref[...] = (bg_ref[...] * (w0 * prev + w1 * p + w2 * nxt)).astype(o_ref.dtype)


def _conv_b(p, bg, w_conv, l, n_prompt_rows, seq, dec_seq, tr, tc):
    m, ch = p.shape
    assert tr == seq and tr % GRID_W == 0 and dec_seq % tr == 0
    hb = tr // GRID_W
    n_hblocks = m // GRID_W
    return pl.pallas_call(
        functools.partial(_conv_b_kernel, n_prompt_tiles=n_prompt_rows // tr, tiles_per_img=dec_seq // tr),
        grid=(m // tr, ch // tc),
        in_specs=[pl.BlockSpec((tr, tc), lambda i, j: (i, j)),
                  pl.BlockSpec((GRID_W, tc), lambda i, j: (jnp.maximum(i * hb - 1, 0), j)),
                  pl.BlockSpec((GRID_W, tc), lambda i, j: (jnp.minimum((i + 1) * hb, n_hblocks - 1), j)),
                  pl.BlockSpec((tr, tc), lambda i, j: (i, j)),
                  pl.BlockSpec((None, 3, tc), lambda i, j: (l, 0, j))],
        out_specs=pl.BlockSpec((tr, tc), lambda i, j: (i, j)),
        out_shape=jax.ShapeDtypeStruct((m, ch), BF16), name="conv_b",
        compiler_params=_params(("parallel", "parallel")),
    )(p, p, p, bg, w_conv)


def _shift_conv_kernel(x_ref, lo_ref, hi_ref, w_ref, o_ref, *, n_prompt_tiles, tiles_per_seq):
    i = pl.program_id(0)
    t = (i - n_prompt_tiles) % tiles_per_seq
    is_prompt = i < n_prompt_tiles
    m_lo = jnp.where(jnp.logical_or(is_prompt, t == 0), 0.0, 1.0)
    m_hi = jnp.where(jnp.logical_or(is_prompt, t == tiles_per_seq - 1), 0.0, 1.0)
    x = x_ref[...]
    prev, nxt = _shift_rows(x, lo_ref[pl.ds(SUBLANES - 1, 1), :] * m_lo, hi_ref[pl.ds(0, 1), :] * m_hi)
    o_ref[...] = (w_ref[pl.ds(0, 1), :] * prev + w_ref[pl.ds(1, 1), :] * x
                  + w_ref[pl.ds(2, 1), :] * nxt).astype(o_ref.dtype)


def _shift_conv(x, w_conv, n_prompt_rows, seq, dec_seq, tr, tc):
    m, ch = x.shape
    assert tr == seq and dec_seq % tr == 0
    hb = tr // SUBLANES
    n_hblocks = m // SUBLANES
    return pl.pallas_call(
        functools.partial(_shift_conv_kernel, n_prompt_tiles=n_prompt_rows // tr, tiles_per_seq=dec_seq // tr),
        grid=(m // tr, ch // tc),
        in_specs=[pl.BlockSpec((tr, tc), lambda i, j: (i, j)),
                  pl.BlockSpec((SUBLANES, tc), lambda i, j: (jnp.maximum(i * hb - 1, 0), j)),
                  pl.BlockSpec((SUBLANES, tc), lambda i, j: (jnp.minimum((i + 1) * hb, n_hblocks - 1), j)),
                  pl.BlockSpec((3, tc), lambda i, j: (0, j))],
        out_specs=pl.BlockSpec((tr, tc), lambda i, j: (i, j)),
        out_shape=jax.ShapeDtypeStruct((m, ch), F32), name="shift_conv",
        compiler_params=_params(("parallel", "parallel")),
    )(x, x, x, w_conv)


def _allsum_groups(x, width):
    s = width
    while s < LANES:
        x = x + pltpu.roll(x, s, axis=x.ndim - 1)
        s *= 2
    return x


def _scan_kernel(blk_ref, sid_ref, edge_ref, r_ref, k_ref, v_ref, w_ref, a_ref, kk_ref, ka_ref, rk_ref, s0_ref,
                 y_ref, c_ref, sf_ref, s_ref, sa_ref, ekk_ref, ew_ref, eb_ref, ekd_ref, er_ref,
                 *, steps, n_zero_init, n_heads):
    del blk_ref
    d = pl.program_id(0)
    g = pl.program_id(1)
    rows = r_ref.shape[1]
    n_grp = LANES // n_heads
    first = edge_ref[0, g] == 1
    last = edge_ref[1, g] == 1
    from_zero = sid_ref[g] < n_zero_init

    @pl.when(jnp.logical_and(first, from_zero))
    def _():
        s_ref[...] = jnp.zeros_like(s_ref)

    @pl.when(jnp.logical_and(first, jnp.logical_not(from_zero)))
    def _():
        s_ref[...] = s0_ref[...]

    flat = (steps * rows, LANES)
    tile = (steps, rows, LANES)

    def head_sum(z):
        return jnp.sum(_allsum_groups(z.reshape(flat), n_heads).reshape(tile), axis=1, keepdims=True)

    k_t = k_ref[...]
    a_t = a_ref[...]
    r_t = r_ref[...]
    kk = k_t * kk_ref[...][None]
    kk = kk / jnp.maximum(jnp.sqrt(head_sum(kk * kk)), 1e-12)
    kd = k_t * (1.0 + (a_t - 1.0) * ka_ref[...][None])
    c_ref[...] = head_sum(r_t * kd * rk_ref[...][None])
    lane_grp = lax.broadcasted_iota(jnp.int32, flat, 1) // n_heads
    for src, dst in ((kk, ekk_ref), (w_ref[...], ew_ref), (kk * a_t, eb_ref), (kd, ekd_ref), (r_t, er_ref)):
        src = src.reshape(flat)
        shifted = [src] + [pltpu.roll(src, s * n_heads, axis=1) for s in range(1, n_grp)]
        for p in range(n_grp):
            e = shifted[(n_grp - 1 - p) % n_grp]
            for g in range(n_grp - 1):
                e = jnp.where(lane_grp == g, shifted[(g - p) % n_grp], e)
            dst[:, pl.ds(p * rows, rows), :] = e.reshape(tile)

    def step(n, carry):
        tt = jnp.where(d == 0, n, steps - 1 - n)
        e_kk = ekk_ref[tt]
        e_w = ew_ref[tt]
        e_b = eb_ref[tt]
        e_kd = ekd_ref[tt]
        e_r = er_ref[tt]

        def read_out(jv, carry2):
            sa_ref[pl.ds(jv, 1), :] = jnp.sum(s_ref[jv] * e_kk, axis=0, keepdims=True)
            return carry2

        lax.fori_loop(0, rows, read_out, 0, unroll=2 * ROWS_PER_ITER)

        def update(jv, carry2):
            sn = s_ref[jv] * e_w - sa_ref[pl.ds(jv, 1), :] * e_b + v_ref[tt, pl.ds(jv, 1), :] * e_kd
            s_ref[jv] = sn
            y_ref[tt, pl.ds(jv, 1), :] = jnp.sum(sn * e_r, axis=0, keepdims=True)
            return carry2

        lax.fori_loop(0, rows, update, 0, unroll=ROWS_PER_ITER)
        return carry

    lax.fori_loop(0, steps, step, 0)

    @pl.when(last)
    def _():
        sf_ref[...] = s_ref[...]


def _wkv_scan(rkv, dec, lr, k_k, k_a, r_k, s0, seq_lens, steps, n_heads):
    m = rkv.shape[0]
    rows = rkv.shape[1] // 3
    n_seq = len(seq_lens)
    n_zero_init = n_seq - s0.shape[0]
    blk, sid, edge = [[], []], [], [[], []]
    base = 0
    for s, ln in enumerate(seq_lens):
        n = ln // steps
        blk[0] += [base + c for c in range(n)]
        blk[1] += [base + n - 1 - c for c in range(n)]
        sid += [s] * n
        edge[0] += [1] + [0] * (n - 1)
        edge[1] += [0] * (n - 1) + [1]
        base += n
    n_chunks = base
    blk = jnp.asarray(blk, jnp.int32)
    sid = jnp.asarray(sid, jnp.int32)
    edge = jnp.asarray(edge, jnp.int32)

    tile = (steps, rows, LANES)
    par = pl.BlockSpec((rows, LANES), lambda d, g, blk, sid, edge: (0, 0))
    grid_spec = pltpu.PrefetchScalarGridSpec(
        num_scalar_prefetch=3, grid=(2, n_chunks),
        in_specs=[pl.BlockSpec(tile, lambda d, g, blk, sid, edge: (blk[d, g], 0, 0)),
                  pl.BlockSpec(tile, lambda d, g, blk, sid, edge: (blk[d, g], 1, 0)),
                  pl.BlockSpec(tile, lambda d, g, blk, sid, edge: (blk[d, g], 2, 0)),
                  pl.BlockSpec(tile, lambda d, g, blk, sid, edge: (blk[d, g], d, 0)),
                  pl.BlockSpec(tile, lambda d, g, blk, sid, edge: (blk[d, g], d, 0)),
                  par, par, par,
                  pl.BlockSpec((None, None, rows, HEAD, LANES),
                               lambda d, g, blk, sid, edge: (jnp.maximum(sid[g] - n_zero_init, 0), d, 0, 0, 0))],
        out_specs=[pl.BlockSpec((None, steps, rows, LANES), lambda d, g, blk, sid, edge: (d, blk[d, g], 0, 0)),
                   pl.BlockSpec((None, steps, 1, LANES), lambda d, g, blk, sid, edge: (d, blk[d, g], 0, 0)),
                   pl.BlockSpec((None, None, rows, HEAD, LANES),
                                lambda d, g, blk, sid, edge: (sid[g], d, 0, 0, 0))],
        scratch_shapes=[pltpu.VMEM((rows, HEAD, LANES), F32), pltpu.VMEM((rows, LANES), F32)]
        + [pltpu.VMEM((steps, HEAD, LANES), F32)] * 5)
    return pl.pallas_call(
        functools.partial(_scan_kernel, steps=steps, n_zero_init=n_zero_init, n_heads=n_heads),
        grid_spec=grid_spec,
        out_shape=[jax.ShapeDtypeStruct((2, m, rows, LANES), F32),
                   jax.ShapeDtypeStruct((2, m, 1, LANES), F32),
                   jax.ShapeDtypeStruct((n_seq, 2, rows, HEAD, LANES), F32)],
        name="wkv_scan",
        compiler_params=_params(("arbitrary", "arbitrary")),
    )(blk, sid, edge, rkv, rkv, rkv, dec, lr, k_k, k_a, r_k, s0)


def _scan_post_kernel(y_ref, c_ref, v_ref, g_ref, gg_ref, gb_ref, o_ref, *, n_heads):
    steps, rows, _ = v_ref.shape
    y = y_ref[0] + y_ref[1]
    flat = (steps * rows, LANES)

    def head_mean(z):
        tot = _allsum_groups(z.reshape(flat), n_heads).reshape(steps, rows, LANES)
        return jnp.sum(tot, axis=1, keepdims=True) * (1.0 / HEAD)

    mu = head_mean(y)
    cen = y - mu
    var = head_mean(cen * cen)
    yn = cen * lax.rsqrt(var + GN_EPS) * gg_ref[...][None] + gb_ref[...][None]
    yn = yn + (c_ref[0] + c_ref[1]) * v_ref[...]
    o_ref[...] = (yn * g_ref[...]).astype(o_ref.dtype)


def _scan_post(y, cb, rkv, gate, gn_g, gn_b, steps, n_heads):
    m = rkv.shape[0]
    rows = rkv.shape[1] // 3
    par = pl.BlockSpec((rows, LANES), lambda i: (0, 0))
    return pl.pallas_call(
        functools.partial(_scan_post_kernel, n_heads=n_heads),
        grid=(m // steps,),
        in_specs=[pl.BlockSpec((2, steps, rows, LANES), lambda i: (0, i, 0, 0)),
                  pl.BlockSpec((2, steps, 1, LANES), lambda i: (0, i, 0, 0)),
                  pl.BlockSpec((steps, rows, LANES), lambda i: (i, 2, 0)),
                  pl.BlockSpec((steps, rows, LANES), lambda i: (i, 0, 0)),
                  par, par],
        out_specs=pl.BlockSpec((steps, rows, LANES), lambda i: (i, 0, 0)),
        out_shape=jax.ShapeDtypeStruct((m, rows, LANES), BF16), name="wkv_post",
        compiler_params=_params(("parallel",)),
    )(y, cb, rkv, gate, gn_g, gn_b)


def _perm_last(w, n_heads):
    lead = w.shape[:-1]
    w = w.reshape(lead + (-1, n_heads, HEAD))
    return jnp.swapaxes(w, -1, -2).reshape(lead + (-1,))


def _state_to_tiles(s, n_heads):
    n = s.shape[0]
    n_grp = LANES // n_heads
    rows = HEAD // n_grp
    s = s.reshape(n, 2, n_heads, rows, n_grp, rows, n_grp)
    return s.transpose(0, 1, 3, 6, 5, 4, 2).reshape(n, 2, rows, HEAD, LANES)


def _tiles_to_state(t, n_heads):
    n = t.shape[0]
    n_grp = LANES // n_heads
    rows = HEAD // n_grp
    t = t.reshape(n, 2, rows, n_grp, rows, n_grp, n_heads)
    return t.transpose(0, 1, 6, 2, 5, 4, 3).reshape(n, 2, n_heads, HEAD, HEAD)


def kernel(x_prompt, x_sample, state_rwkv, c, c_ctx, w_ada, b_ada, norm_g, w_ffn_gate, w_ffn_up,
           w_ffn_down, w_in, w_conv_a, ln_a_g, ln_a_b, w_out_a, w_conv_b, w_out_b, w_shift, k_k,
           k_a, r_k, w0, w_w2, a0, w_a2, w_g2, gn_g, gn_b, w_out_c, w_o, final_g):
    batch, seq, d = x_prompt.shape
    dec_batch, dec_seq, _ = x_sample.shape
    depth = w_ada.shape[0]
    d_ff = w_ffn_gate.shape[-1]
    w_a = w_out_a.shape[1]
    w_b = w_out_b.shape[1]
    w_c = w_out_c.shape[1]
    n_heads = w_c // HEAD
    lora_w = w_w2.shape[2]
    lora_a = w_a2.shape[2]
    lora_g = w_g2.shape[1]
    n_prompt = batch * seq
    m = n_prompt + dec_batch * dec_seq
    assert seq & (seq - 1) == 0 and GRID_W & (GRID_W - 1) == 0 and dec_seq % GRID_W == 0
    assert lora_w == LANES and lora_a == LANES and LANES % n_heads == 0 and w_c % LANES == 0
    rows_c = w_c // LANES

    off_a = 0
    off_b = off_a + 2 * w_a
    off_rkv = off_b + 3 * w_b
    off_lw = off_rkv + 3 * w_c
    off_la = off_lw + 2 * lora_w
    off_lg = off_la + 2 * lora_a
    off_g = off_lg + lora_g
    lg_pad = -(-lora_g // LANES) * LANES

    def grp(i, rows):
        r0 = i * rows
        return jnp.where(r0 < n_prompt, 0, 1 + (r0 - n_prompt) // dec_seq)

    n_cond = 1 + dec_batch
    rpad = -(-n_cond // SUBLANES) * SUBLANES
    cond = jnp.concatenate([c_ctx[None, :], c, jnp.zeros((rpad - n_cond, d), F32)], axis=0)
    mod = _adaln_all(cond, w_ada, b_ada).reshape(depth * rpad * N_MOD, 1, d)

    def mod_row(l, g, idx):
        return (l * rpad + g) * N_MOD + idx

    norm_g3 = norm_g.reshape(depth * 3, 1, d)
    x = jnp.concatenate([x_prompt.reshape(n_prompt, d), x_sample.reshape(m - n_prompt, d)], axis=0)

    tr = seq
    row_gcd = math.gcd(n_prompt, dec_seq)
    tm_mm = _pick(row_gcd, (1024, 512, 256, 128))
    tm_big = _pick(row_gcd, (2048, 1024, 512, 256, 128))

    def mod_block(l, idx, rows, tn):
        return (mod, (None, 1, tn), lambda i, j, k: (mod_row(l, grp(i, rows), idx), 0, j))

    def ffn_half(x, l, half):
        base = 0 if half == 0 else 6
        h = _norm_mod(x, norm_g3, l * 3 + (0 if half == 0 else 2), mod,
                      lambda i, which: mod_row(l, grp(i, tr), base + which), BF16, tr)
        tn = _pick(d_ff, (256, 128))
        wg = (w_ffn_gate, (None, None, d, tn), lambda i, j, k: (l, half, 0, j))
        wu = (w_ffn_up, (None, None, d, tn), lambda i, j, k: (l, half, 0, j))
        (hg,) = _fused_matmul(
            "ffn_up", h, (tm_mm, d), lambda i, j, k: (i, 0), [wg, wu], [], [(d_ff, BF16)],
            (m // tm_mm, d_ff // tn, 1), tm_mm, tn,
            lambda accs, ex: [_silu(accs[0]) * accs[1]])
        tk, tail = _split_k(d_ff, (1536, 1024, 512, 256, 128))
        nk = d_ff // tk
        tn2 = _pick(d, (512, 256, 128))
        wd = (w_ffn_down, (None, None, tk, tn2), lambda i, j, k: (l, half, k, j))
        xt = (x, (tm_big, tn2), lambda i, j, k: (i, j))
        tail_ops = None
        if tail:
            tb = (d_ff - tail) // tail
            tail_ops = (hg, (tm_big, tail), lambda i, j, k: (i, tb),
                        w_ffn_down, (None, None, tail, tn2), lambda i, j, k: (l, half, tb, j))
        (x_new,) = _fused_matmul(
            "ffn_down", hg, (tm_big, tk), lambda i, j, k: (i, k), [wd], [xt, mod_block(l, base + 2, tm_big, tn2)],
            [(d, F32)], (m // tm_big, d // tn2, nk), tm_big, tn2,
            lambda accs, ex: [ex[0] + 0.5 * ex[1] * accs[0]], tail=tail_ops)
        return x_new

    def in_proj(name, h, w_arr, w_lead, col0, n_cols, n_w, tn, epilogue, outs, tm_, tk=None):
        tk = d if tk is None else tk
        assert all((col0 + q * n_cols) % tn == 0 for q in range(n_w))
        ws = [(w_arr, (None,) * len(w_lead) + (tk, tn), functools.partial(
            lambda i, j, k, q: w_lead + (k, (col0 + q * n_cols) // tn + j), q=q)) for q in range(n_w)]
        return _fused_matmul(name, h, (tm_, tk), lambda i, j, k: (i, k), ws, [], outs,
                             (m // tm_, n_cols // tn, d // tk), tm_, tn, epilogue)

    def tile_c(p):
        return _perm_last(p.reshape(-1), n_heads).reshape(rows_c, LANES)

    new_states = []
    for l in range(depth):
        x = ffn_half(x, l, 0)

        h = _norm_mod(x, norm_g3, l * 3 + 1, mod, lambda i, which: mod_row(l, grp(i, tr), 3 + which), BF16, tr)
        tn_s = _pick(w_a, (256, 128))
        (glu,) = in_proj("in_glu", h, w_in, (l,), off_a, w_a, 2, tn_s,
                         lambda accs, ex: [accs[0] * _sigmoid(accs[1])], [(w_a, F32)], tm_mm)
        p_b, bg = in_proj("in_b", h, w_in, (l,), off_b, w_b, 3, tn_s,
                          lambda accs, ex: [accs[2] * accs[0], accs[1]], [(w_b, F32), (w_b, F32)], tm_mm,
                          tk=_pick(d, (2048, 1024, 512, 256, 128)))
        tn_l = _pick(w_c, (512, 256, 128))
        w_rkv = _perm_last(lax.slice_in_dim(w_in[l], off_rkv, off_rkv + 3 * w_c, axis=1), n_heads)
        (zrkv,) = in_proj("in_rkv", h, w_rkv, (), 0, 3 * w_c, 1, tn_l, lambda accs, ex: [accs[0]],
                          [(3 * w_c, F32)], tm_mm)
        (lw,) = in_proj("in_lw", h, w_in, (l,), off_lw, 2 * lora_w, 1, 2 * lora_w,
                        lambda accs, ex: [jnp.tanh(accs[0])], [(2 * lora_w, F32)], tm_mm)
        (la,) = in_proj("in_la", h, w_in, (l,), off_la, 2 * lora_a, 1, 2 * lora_a, lambda accs, ex: [accs[0]],
                        [(2 * lora_a, F32)], tm_mm)
        (lg,) = in_proj("in_lg", h, w_in, (l,), off_lg, lg_pad, 1, lg_pad, lambda accs, ex: [_sigmoid(accs[0])],
                        [(lg_pad, BF16)], tm_mm)
        w_gates = lax.slice_in_dim(w_in[l], off_g, off_g + 3 * d, axis=1)
        (sig_g,) = in_proj("in_gates", h, w_gates, (), 0, 3 * d, 1, tn_l, lambda accs, ex: [_sigmoid(accs[0])],
                           [(3 * d, BF16)], tm_mm)

        feat_a = _conv_a(glu, w_conv_a, ln_a_g, ln_a_b, l, n_prompt, seq, tr)
        feat_b = _conv_b(p_b, bg, w_conv_b, l, n_prompt, seq, dec_seq, tr, _pick(w_b, (1024, 512, 256, 128)))

        rkv = _shift_conv(zrkv, _perm_last(w_shift[l], n_heads), n_prompt, seq, dec_seq, tr,
                          _pick(3 * w_c, (1024, 512, 256, 128)))
        tn_c = _pick(w_c, (1024, 512, 256, 128))
        nj_c = w_c // tn_c

        def lora(name, src, w2, bias, epi):
            wspec = (_perm_last(w2, n_heads), (None, LANES, tn_c), lambda i, j, k: (j // nj_c, 0, j % nj_c))
            bspec = (_perm_last(bias, n_heads).reshape(2, 1, w_c), (None, 1, tn_c),
                     lambda i, j, k: (j // nj_c, 0, j % nj_c))
            (o,) = _fused_matmul(name, src, (tm_mm, LANES), lambda i, j, k: (i, j // nj_c), [wspec], [bspec],
                                 [(2 * w_c, F32)], (m // tm_mm, 2 * nj_c, 1), tm_mm, tn_c, epi)
            return o

        decay = lora("decay", lw, w_w2[l], w0[l],
                     lambda accs, ex: [jnp.exp(-DECAY_SCALE * _sigmoid(ex[0] + accs[0]))])
        lr = lora("lr", la, w_a2[l], a0[l], lambda accs, ex: [_sigmoid(ex[0] + accs[0])])
        w_g2p = jnp.pad(_perm_last(w_g2[l], n_heads), ((0, lg_pad - lora_g), (0, 0)))
        (gate_c,) = _fused_matmul("gate_c", lg, (tm_mm, lg_pad), lambda i, j, k: (i, 0),
                                  [(w_g2p, (lg_pad, tn_c), lambda i, j, k: (0, j))], [], [(w_c, F32)],
                                  (m // tm_mm, nj_c, 1), tm_mm, tn_c, lambda accs, ex: [accs[0]])

        rkv3 = rkv.reshape(m, 3 * rows_c, LANES)
        dec3 = decay.reshape(m, 2 * rows_c, LANES)
        lr3 = lr.reshape(m, 2 * rows_c, LANES)
        kk_t, ka_t, rk_t = tile_c(k_k[l]), tile_c(k_a[l]), tile_c(r_k[l])
        steps = _pick(math.gcd(seq, dec_seq), (64, 32, 16, 8))
        y_raw, cb, s_fin = _wkv_scan(rkv3, dec3, lr3, kk_t, ka_t, rk_t, _state_to_tiles(state_rwkv[:, l], n_heads),
                                     (seq,) * batch + (dec_seq,) * dec_batch, steps, n_heads)
        new_states.append(s_fin[:batch])
        feat_c = _scan_post(y_raw, cb, rkv3, gate_c.reshape(m, rows_c, LANES), tile_c(gn_g[l]), tile_c(gn_b[l]),
                            steps, n_heads).reshape(m, w_c)

        feat = jnp.concatenate([feat_a, feat_b, feat_c], axis=1)
        w_out_c_perm = jnp.swapaxes(w_out_c[l].reshape(n_heads, HEAD, d), 0, 1).reshape(w_c, d)
        w_cat = jnp.concatenate([w_out_a[l], w_out_b[l], w_out_c_perm], axis=0)
        tn_m = _pick(d, (512, 256, 128))
        tk_m = _pick(math.gcd(math.gcd(w_a, w_b), w_c), (1024, 512, 256, 128))
        merged = _merge(feat, w_cat, sig_g, (w_a, w_b, w_c), d, tm_big, tn_m, tk_m)
        tk = _pick(d, (1024, 512, 256, 128))
        (x,) = _fused_matmul(
            "w_o", merged, (tm_big, tk), lambda i, j, k: (i, k), [(w_o, (None, tk, tn_m), lambda i, j, k: (l, k, j))],
            [(x, (tm_big, tn_m), lambda i, j, k: (i, j)), mod_block(l, 5, tm_big, tn_m)], [(d, F32)],
            (m // tm_big, d // tn_m, d // tk), tm_big, tn_m, lambda accs, ex: [ex[0] + ex[1] * accs[0]])

        x = ffn_half(x, l, 1)

    y = _norm_mod(x, final_g.reshape(1, 1, d), 0, None, None, F32, tr)
    y_prompt = y[:n_prompt].reshape(batch, seq, d)
    y_sample = y[n_prompt:].reshape(dec_batch, dec_seq, d)
    st = jnp.stack([_tiles_to_state(s, n_heads) for s in new_states], axis=1)
    return (y_prompt, y_sample, st)
```

```python
import functools
import math

import jax
import jax.numpy as jnp
from jax import lax
from jax.experimental import pallas as pl
from jax.experimental.pallas import tpu as pltpu

F32 = jnp.float32
BF16 = jnp.bfloat16

GRID_W = 64
HEAD = 64
N_MOD = 9
RMS_EPS = 1e-6
LN_EPS = 1e-5
GN_EPS = HEAD * 1e-5
DECAY_SCALE = 0.6065306597126334
LANES = 128
SUBLANES = 8
VMEM_LIMIT = 56 * 1024 * 1024


def _pick(n, prefs):
    for p in prefs:
        if n % p == 0:
            return p
    raise ValueError(f"no tile in {prefs} divides {n}")


def _split_k(k, prefs):
    for tk in prefs:
        tail = k % tk
        if k >= tk and (tail == 0 or (tail % LANES == 0 and (k - tail) % tail == 0)):
            return tk, tail
    raise ValueError(f"no K tiling for {k}")


def _params(sem):
    return pltpu.CompilerParams(dimension_semantics=sem, vmem_limit_bytes=VMEM_LIMIT)


def _sigmoid(x):
    return 1.0 / (1.0 + jnp.exp(-x))


def _silu(x):
    return x * _sigmoid(x)


def _mm_kernel(*refs, n_w, n_t, n_e, n_o, nk, epilogue, a_prologue):
    a_ref = refs[0]
    w_refs = refs[1:1 + n_w]
    pos = 1 + n_w
    t_refs = refs[pos:pos + 2 * n_t]
    pos += 2 * n_t
    e_refs = refs[pos:pos + n_e]
    pos += n_e
    o_refs = refs[pos:pos + n_o]
    acc_refs = refs[pos + n_o:]
    a = a_ref[...]
    if a_prologue is not None:
        a = a_prologue(a)
    a = a.astype(BF16)
    prods = [jnp.dot(a, w[...].astype(BF16), preferred_element_type=F32) for w in w_refs]

    def first(ps):
        if n_t:
            tail = jnp.dot(t_refs[0][...].astype(BF16), t_refs[1][...].astype(BF16), preferred_element_type=F32)
            return [ps[0] + tail] + ps[1:]
        return ps

    def finish(accs):
        outs = epilogue(accs, [e[...] for e in e_refs])
        for o, val in zip(o_refs, outs):
            o[...] = val.astype(o.dtype)

    if nk == 1:
        finish(first(prods))
        return
    k = pl.program_id(2)

    @pl.when(k == 0)
    def _():
        for acc, p in zip(acc_refs, first(prods)):
            acc[...] = p

    @pl.when(k > 0)
    def _():
        for acc, p in zip(acc_refs, prods):
            acc[...] += p

    @pl.when(k == nk - 1)
    def _():
        finish([acc[...] for acc in acc_refs])


def _fused_matmul(name, a, a_block, a_map, ws, extras, outs, grid, tm, tn, epilogue, a_prologue=None, tail=None):
    nk = grid[2]
    m_rows = grid[0] * tm
    in_specs = [pl.BlockSpec(a_block, a_map)]
    args = [a]
    operands = list(ws)
    if tail is not None:
        operands += [tail[0:3], tail[3:6]]
    for arr, blk, mp in operands + list(extras):
        in_specs.append(pl.BlockSpec(blk, mp))
        args.append(arr)
    out_specs = [pl.BlockSpec((tm, tn), lambda i, j, k: (i, j)) for _ in outs]
    out_shape = [jax.ShapeDtypeStruct((m_rows, n), dt) for n, dt in outs]
    scratch = [pltpu.VMEM((tm, tn), F32) for _ in ws] if nk > 1 else []
    kern = functools.partial(_mm_kernel, n_w=len(ws), n_t=0 if tail is None else 1, n_e=len(extras),
                             n_o=len(outs), nk=nk, epilogue=epilogue, a_prologue=a_prologue)
    return pl.pallas_call(
        kern, grid=grid, in_specs=in_specs, out_specs=out_specs, out_shape=out_shape,
        scratch_shapes=scratch, name=name,
        compiler_params=_params(("parallel", "parallel", "arbitrary")),
    )(*args)


def _merge_kernel(a_ref, w0, w1, w2, g0, g1, g2, o_ref, acc0, acc1, acc2, *, bounds):
    k = pl.program_id(2)
    lo = 0
    for w_ref, acc, hi in zip((w0, w1, w2), (acc0, acc1, acc2), bounds):
        @pl.when(jnp.logical_and(k >= lo, k < hi))
        def _(w_ref=w_ref, acc=acc, lo=lo):
            p = jnp.dot(a_ref[...], w_ref[...].astype(BF16), preferred_element_type=F32)

            @pl.when(k == lo)
            def _():
                acc[...] = p

            @pl.when(k > lo)
            def _():
                acc[...] += p
        lo = hi

    @pl.when(k == bounds[-1] - 1)
    def _():
        o_ref[...] = (g0[...].astype(F32) * acc0[...] + g1[...].astype(F32) * acc1[...]
                      + g2[...].astype(F32) * acc2[...]).astype(o_ref.dtype)


def _merge(feat, w_outs, l, sig_g, d, tm, tn, tk):
    m = feat.shape[0]
    bounds, lows, acc = [], [], 0
    for w in w_outs:
        assert w.shape[-2] % tk == 0
        lows.append(acc)
        acc += w.shape[-2] // tk
        bounds.append(acc)
    assert acc * tk == feat.shape[1]
    nj = d // tn

    def kmap(q):
        return lambda k: jnp.clip(k - lows[q], 0, bounds[q] - lows[q] - 1)

    w_specs = []
    for q, w in enumerate(w_outs):
        if w.ndim == 3:
            w_specs.append(pl.BlockSpec((None, tk, tn), functools.partial(lambda i, j, k, q: (l, kmap(q)(k), j), q=q)))
        else:
            w_specs.append(pl.BlockSpec((tk, tn), functools.partial(lambda i, j, k, q: (kmap(q)(k), j), q=q)))
    gates = [pl.BlockSpec((tm, tn), functools.partial(lambda i, j, k, q: (i, q * nj + j), q=q)) for q in range(3)]
    return pl.pallas_call(
        functools.partial(_merge_kernel, bounds=tuple(bounds)),
        grid=(m // tm, nj, bounds[-1]),
        in_specs=[pl.BlockSpec((tm, tk), lambda i, j, k: (i, k))] + w_specs + gates,
        out_specs=pl.BlockSpec((tm, tn), lambda i, j, k: (i, j)),
        out_shape=jax.ShapeDtypeStruct((m, d), BF16),
        scratch_shapes=[pltpu.VMEM((tm, tn), F32)] * 3, name="merge",
        compiler_params=_params(("parallel", "parallel", "arbitrary")),
    )(feat, *w_outs, sig_g, sig_g, sig_g)


def _ada_kernel(c_ref, w_ref, b_ref, o_ref):
    c = _silu(c_ref[...]).astype(BF16)
    o_ref[...] = jnp.dot(c, w_ref[...].astype(BF16), preferred_element_type=F32) + b_ref[...]


def _adaln_all(cond, w_ada, b_ada):
    depth, d, n = w_ada.shape
    r = cond.shape[0]
    tn = _pick(n, (512, 256, 128))
    return pl.pallas_call(
        _ada_kernel,
        grid=(depth, n // tn),
        in_specs=[pl.BlockSpec((r, d), lambda l, j: (0, 0)),
                  pl.BlockSpec((None, d, tn), lambda l, j: (l, 0, j)),
                  pl.BlockSpec((None, 1, tn), lambda l, j: (l, 0, j))],
        out_specs=pl.BlockSpec((None, r, tn), lambda l, j: (l, 0, j)),
        out_shape=jax.ShapeDtypeStruct((depth, r, n), F32), name="adaln",
        compiler_params=_params(("parallel", "parallel")),
    )(cond, w_ada, b_ada.reshape(depth, 1, n))


def _norm_kernel(x_ref, g_ref, *rest, modulate):
    if modulate:
        sh_ref, sc_ref, o_ref = rest
    else:
        (o_ref,) = rest
    x = x_ref[...]
    ms = jnp.mean(x * x, axis=-1, keepdims=True)
    y = x * lax.rsqrt(ms + RMS_EPS) * g_ref[...]
    if modulate:
        y = y * (1.0 + sc_ref[...]) + sh_ref[...]
    o_ref[...] = y.astype(o_ref.dtype)


def _norm_mod(x, g, g_idx, mod, mod_idx, out_dtype, tr):
    m, d = x.shape
    in_specs = [pl.BlockSpec((tr, d), lambda i: (i, 0)),
                pl.BlockSpec((None, 1, d), lambda i: (g_idx, 0, 0))]
    args = [x, g]
    if mod is not None:
        in_specs += [pl.BlockSpec((None, 1, d), lambda i: (mod_idx(i, 0), 0, 0)),
                     pl.BlockSpec((None, 1, d), lambda i: (mod_idx(i, 1), 0, 0))]
        args += [mod, mod]
    return pl.pallas_call(
        functools.partial(_norm_kernel, modulate=mod is not None),
        grid=(m // tr,), in_specs=in_specs,
        out_specs=pl.BlockSpec((tr, d), lambda i: (i, 0)),
        out_shape=jax.ShapeDtypeStruct((m, d), out_dtype), name="rmsnorm",
        compiler_params=_params(("parallel",)),
    )(*args)


def _conv_a_kernel(g_ref, w_ref, lg_ref, lb_ref, o_ref, pad_ref, *, n_prompt_tiles, seq, taps, halo):
    rows, ch = g_ref.shape
    half = taps // 2
    seg = jnp.where(pl.program_id(0) < n_prompt_tiles, seq, GRID_W)
    zeros = jnp.zeros((halo, ch), F32)
    pad_ref[pl.ds(0, halo), :] = zeros
    pad_ref[pl.ds(halo + rows, halo), :] = zeros
    pad_ref[pl.ds(halo, rows), :] = g_ref[...]
    pos = lax.broadcasted_iota(jnp.int32, (rows, 1), 0) & (seg - 1)
    acc = jnp.zeros((rows, ch), F32)
    for j in range(taps):
        off = j - half
        valid = jnp.logical_and(pos + off >= 0, pos + off < seg)
        xs = pad_ref[pl.ds(halo + off, rows), :]
        acc = acc + jnp.where(valid, xs, 0.0) * w_ref[pl.ds(j, 1), :]
    mu = jnp.mean(acc, axis=-1, keepdims=True)
    cen = acc - mu
    var = jnp.mean(cen * cen, axis=-1, keepdims=True)
    y = cen * lax.rsqrt(var + LN_EPS) * lg_ref[...] + lb_ref[...]
    o_ref[...] = _silu(y).astype(o_ref.dtype)


def _conv_a(g, w_conv, ln_g, ln_b, l, n_prompt_rows, seq, tr):
    m, ch = g.shape
    taps = w_conv.shape[1]
    halo = 2 * SUBLANES
    assert taps // 2 < halo and tr % seq == 0 and tr % GRID_W == 0
    return pl.pallas_call(
        functools.partial(_conv_a_kernel, n_prompt_tiles=n_prompt_rows // tr, seq=seq, taps=taps, halo=halo),
        grid=(m // tr,),
        in_specs=[pl.BlockSpec((tr, ch), lambda i: (i, 0)),
                  pl.BlockSpec((None, taps, ch), lambda i: (l, 0, 0)),
                  pl.BlockSpec((None, 1, ch), lambda i: (l, 0, 0)),
                  pl.BlockSpec((None, 1, ch), lambda i: (l, 0, 0))],
        out_specs=pl.BlockSpec((tr, ch), lambda i: (i, 0)),
        out_shape=jax.ShapeDtypeStruct((m, ch), BF16),
        scratch_shapes=[pltpu.VMEM((tr + 2 * halo, ch), F32)], name="conv_a",
        compiler_params=_params(("parallel",)),
    )(g, w_conv, ln_g.reshape(-1, 1, ch), ln_b.reshape(-1, 1, ch))


def _shift_rows(x, lo_row, hi_row):
    rows = x.shape[0]
    r = lax.broadcasted_iota(jnp.int32, (rows, 1), 0)
    prev = jnp.where(r == 0, lo_row, pltpu.roll(x, 1, axis=0))
    nxt = jnp.where(r == rows - 1, hi_row, pltpu.roll(x, rows - 1, axis=0))
    return prev, nxt


def _conv_b_kernel(p_ref, lo_ref, hi_ref, bg_ref, w_ref, o_ref, *, n_prompt_tiles, tiles_per_img):
    rows, ch = p_ref.shape
    i = pl.program_id(0)
    p = p_ref[...]
    w0 = w_ref[pl.ds(0, 1), :]
    w1 = w_ref[pl.ds(1, 1), :]
    w2 = w_ref[pl.ds(2, 1), :]

    @pl.when(i < n_prompt_tiles)
    def _():
        zero = jnp.zeros((1, ch), F32)
        prev, nxt = _shift_rows(p, zero, zero)
        o_ref[...] = (bg_ref[...] * (w0 * prev + w1 * p + w2 * nxt)).astype(o_ref.dtype)

    @pl.when(i >= n_prompt_tiles)
    def _():
        t = (i - n_prompt_tiles) % tiles_per_img
        m_lo = jnp.where(t == 0, 0.0, 1.0)
        m_hi = jnp.where(t == tiles_per_img - 1, 0.0, 1.0)
        prev = jnp.concatenate([lo_ref[...] * m_lo, p[:rows - GRID_W]], axis=0)
        nxt = jnp.concatenate([p[GRID_W:], hi_ref[...] * m_hi], axis=0)
        o_ref[...] = (bg_ref[...] * (w0 * prev + w1 * p + w2 * nxt)).astype(o_ref.dtype)


def _conv_b(p, bg, w_conv, l, n_prompt_rows, seq, dec_seq, tr, tc):
    m, ch = p.shape
    assert tr == seq and tr % GRID_W == 0 and dec_seq % tr == 0
    hb = tr // GRID_W
    n_hblocks = m // GRID_W
    return pl.pallas_call(
        functools.partial(_conv_b_kernel, n_prompt_tiles=n_prompt_rows // tr, tiles_per_img=dec_seq // tr),
        grid=(m // tr, ch // tc),
        in_specs=[pl.BlockSpec((tr, tc), lambda i, j: (i, j)),
                  pl.BlockSpec((GRID_W, tc), lambda i, j: (jnp.maximum(i * hb - 1, 0), j)),
                  pl.BlockSpec((GRID_W, tc), lambda i, j: (jnp.minimum((i + 1) * hb, n_hblocks - 1), j)),
                  pl.BlockSpec((tr, tc), lambda i, j: (i, j)),
                  pl.BlockSpec((None, 3, tc), lambda i, j: (l, 0, j))],
        out_specs=pl.BlockSpec((tr, tc), lambda i, j: (i, j)),
        out_shape=jax.ShapeDtypeStruct((m, ch), BF16), name="conv_b",
        compiler_params=_params(("parallel", "parallel")),
    )(p, p, p, bg, w_conv)


def _shift_conv_kernel(x_ref, lo_ref, hi_ref, w_ref, o_ref, *, n_prompt_tiles, tiles_per_seq):
    i = pl.program_id(0)
    t = (i - n_prompt_tiles) % tiles_per_seq
    is_prompt = i < n_prompt_tiles
    m_lo = jnp.where(jnp.logical_or(is_prompt, t == 0), 0.0, 1.0)
    m_hi = jnp.where(jnp.logical_or(is_prompt, t == tiles_per_seq - 1), 0.0, 1.0)
    x = x_ref[...]
    prev, nxt = _shift_rows(x, lo_ref[pl.ds(SUBLANES - 1, 1), :] * m_lo, hi_ref[pl.ds(0, 1), :] * m_hi)
    o_ref[...] = (w_ref[pl.ds(0, 1), :] * prev + w_ref[pl.ds(1, 1), :] * x
                  + w_ref[pl.ds(2, 1), :] * nxt).astype(o_ref.dtype)


def _shift_conv(x, w_conv, n_prompt_rows, seq, dec_seq, tr, tc):
    m, ch = x.shape
    assert tr == seq and dec_seq % tr == 0
    hb = tr // SUBLANES
    n_hblocks = m // SUBLANES
    return pl.pallas_call(
        functools.partial(_shift_conv_kernel, n_prompt_tiles=n_prompt_rows // tr, tiles_per_seq=dec_seq // tr),
        grid=(m // tr, ch // tc),
        in_specs=[pl.BlockSpec((tr, tc), lambda i, j: (i, j)),
                  pl.BlockSpec((SUBLANES, tc), lambda i, j: (jnp.maximum(i * hb - 1, 0), j)),
                  pl.BlockSpec((SUBLANES, tc), lambda i, j: (jnp.minimum((i + 1) * hb, n_hblocks - 1), j)),
                  pl.BlockSpec((3, tc), lambda i, j: (0, j))],
        out_specs=pl.BlockSpec((tr, tc), lambda i, j: (i, j)),
        out_shape=jax.ShapeDtypeStruct((m, ch), F32), name="shift_conv",
        compiler_params=_params(("parallel", "parallel")),
    )(x, x, x, w_conv)


def _allsum_groups(x, width):
    s = width
    while s < LANES:
        x = x + pltpu.roll(x, s, axis=x.ndim - 1)
        s *= 2
    return x


def _key_sums(tiles):
    parts = [t.reshape(HEAD // SUBLANES, SUBLANES, LANES).sum(axis=0) for t in tiles]
    sub = lax.broadcasted_iota(jnp.int32, (SUBLANES, LANES), 0)
    half = SUBLANES // 2
    while half >= 1:
        low = (sub & half) == 0
        n = len(parts) // 2
        parts = [jnp.where(low, parts[u] + pltpu.roll(parts[u], SUBLANES - half, axis=0),
                           parts[u + n] + pltpu.roll(parts[u + n], half, axis=0)) for u in range(n)]
        half //= 2
    return parts[0]


def _scan_kernel(blk_ref, sid_ref, edge_ref, r_ref, k_ref, v_ref, w_ref, a_ref, kk_ref, ka_ref, rk_ref, s0_ref,
                 y_ref, c_ref, sf_ref, s_ref, sa_ref, ekk_ref, ew_ref, eb_ref, ekd_ref, er_ref,
                 *, steps, n_zero_init, n_heads):
    del blk_ref
    d = pl.program_id(0)
    g = pl.program_id(1)
    rows = r_ref.shape[1]
    n_grp = LANES // n_heads
    first = edge_ref[0, g] == 1
    last = edge_ref[1, g] == 1
    from_zero = sid_ref[g] < n_zero_init

    @pl.when(jnp.logical_and(first, from_zero))
    def _():
        s_ref[...] = jnp.zeros_like(s_ref)

    @pl.when(jnp.logical_and(first, jnp.logical_not(from_zero)))
    def _():
        s_ref[...] = s0_ref[...]

    flat = (steps * rows, LANES)
    tile = (steps, rows, LANES)

    def head_sum(z):
        return jnp.sum(_allsum_groups(z.reshape(flat), n_heads).reshape(tile), axis=1, keepdims=True)

    k_t = k_ref[...]
    a_t = a_ref[...]
    r_t = r_ref[...]
    kk = k_t * kk_ref[...][None]
    kk = kk / jnp.maximum(jnp.sqrt(head_sum(kk * kk)), 1e-12)
    kd = k_t * (1.0 + (a_t - 1.0) * ka_ref[...][None])
    c_ref[...] = head_sum(r_t * kd * rk_ref[...][None])
    lane_grp = lax.broadcasted_iota(jnp.int32, flat, 1) // n_heads
    for src, dst in ((kk, ekk_ref), (w_ref[...], ew_ref), (kk * a_t, eb_ref), (kd, ekd_ref), (r_t, er_ref)):
        src = src.reshape(flat)
        shifted = [src] + [pltpu.roll(src, s * n_heads, axis=1) for s in range(1, n_grp)]
        for p in range(n_grp):
            e = shifted[(n_grp - 1 - p) % n_grp]
            for g in range(n_grp - 1):
                e = jnp.where(lane_grp == g, shifted[(g - p) % n_grp], e)
            dst[:, pl.ds(p * rows, rows), :] = e.reshape(tile)

    def step(n, carry):
        tt = jnp.where(d == 0, n, steps - 1 - n)
        e_kk = ekk_ref[tt]
        e_w = ew_ref[tt]
        e_b = eb_ref[tt]
        e_kd = ekd_ref[tt]
        e_r = er_ref[tt]

        def read_out(i, carry2):
            base = pl.multiple_of(i * SUBLANES, SUBLANES)
            sa_ref[pl.ds(base, SUBLANES), :] = _key_sums([s_ref[base + u] * e_kk for u in range(SUBLANES)])
            return carry2

        lax.fori_loop(0, rows // SUBLANES, read_out, 0, unroll=2)

        def update(i, carry2):
            base = pl.multiple_of(i * SUBLANES, SUBLANES)
            prods = []
            for u in range(SUBLANES):
                jv = base + u
                sn = s_ref[jv] * e_w - sa_ref[pl.ds(jv, 1), :] * e_b + v_ref[tt, pl.ds(jv, 1), :] * e_kd
                s_ref[jv] = sn
                prods.append(sn * e_r)
            y_ref[tt, pl.ds(base, SUBLANES), :] = _key_sums(prods)
            return carry2

        lax.fori_loop(0, rows // SUBLANES, update, 0)
        return carry

    lax.fori_loop(0, steps, step, 0)

    @pl.when(last)
    def _():
        sf_ref[...] = s_ref[...]


def _wkv_scan(rkv, dec, lr, k_k, k_a, r_k, s0, seq_lens, steps, n_heads):
    m = rkv.shape[0]
    rows = rkv.shape[1] // 3
    n_seq = len(seq_lens)
    n_zero_init = n_seq - s0.shape[0]
    blk, sid, edge = [[], []], [], [[], []]
    base = 0
    for s, ln in enumerate(seq_lens):
        n = ln // steps
        blk[0] += [base + c for c in range(n)]
        blk[1] += [base + n - 1 - c for c in range(n)]
        sid += [s] * n
        edge[0] += [1] + [0] * (n - 1)
        edge[1] += [0] * (n - 1) + [1]
        base += n
    n_chunks = base
    blk = jnp.asarray(blk, jnp.int32)
    sid = jnp.asarray(sid, jnp.int32)
    edge = jnp.asarray(edge, jnp.int32)

    tile = (steps, rows, LANES)
    par = pl.BlockSpec((rows, LANES), lambda d, g, blk, sid, edge: (0, 0))
    grid_spec = pltpu.PrefetchScalarGridSpec(
        num_scalar_prefetch=3, grid=(2, n_chunks),
        in_specs=[pl.BlockSpec(tile, lambda d, g, blk, sid, edge: (blk[d, g], 0, 0)),
                  pl.BlockSpec(tile, lambda d, g, blk, sid, edge: (blk[d, g], 1, 0)),
                  pl.BlockSpec(tile, lambda d, g, blk, sid, edge: (blk[d, g], 2, 0)),
                  pl.BlockSpec(tile, lambda d, g, blk, sid, edge: (blk[d, g], d, 0)),
                  pl.BlockSpec(tile, lambda d, g, blk, sid, edge: (blk[d, g], d, 0)),
                  par, par, par,
                  pl.BlockSpec((None, None, rows, HEAD, LANES),
                               lambda d, g, blk, sid, edge: (jnp.maximum(sid[g] - n_zero_init, 0), d, 0, 0, 0))],
        out_specs=[pl.BlockSpec((None, steps, rows, LANES), lambda d, g, blk, sid, edge: (d, blk[d, g], 0, 0)),
                   pl.BlockSpec((None, steps, 1, LANES), lambda d, g, blk, sid, edge: (d, blk[d, g], 0, 0)),
                   pl.BlockSpec((None, None, rows, HEAD, LANES),
                                lambda d, g, blk, sid, edge: (sid[g], d, 0, 0, 0))],
        scratch_shapes=[pltpu.VMEM((rows, HEAD, LANES), F32), pltpu.VMEM((rows, LANES), F32)]
        + [pltpu.VMEM((steps, HEAD, LANES), F32)] * 5)
    return pl.pallas_call(
        functools.partial(_scan_kernel, steps=steps, n_zero_init=n_zero_init, n_heads=n_heads),
        grid_spec=grid_spec,
        out_shape=[jax.ShapeDtypeStruct((2, m, rows, LANES), F32),
                   jax.ShapeDtypeStruct((2, m, 1, LANES), F32),
                   jax.ShapeDtypeStruct((n_seq, 2, rows, HEAD, LANES), F32)],
        name="wkv_scan",
        compiler_params=_params(("arbitrary", "arbitrary")),
    )(blk, sid, edge, rkv, rkv, rkv, dec, lr, k_k, k_a, r_k, s0)


def _scan_post_kernel(y_ref, c_ref, v_ref, g_ref, gg_ref, gb_ref, o_ref, *, n_heads):
    steps, rows, _ = v_ref.shape
    y = y_ref[0] + y_ref[1]
    flat = (steps * rows, LANES)

    def head_mean(z):
        tot = _allsum_groups(z.reshape(flat), n_heads).reshape(steps, rows, LANES)
        return jnp.sum(tot, axis=1, keepdims=True) * (1.0 / HEAD)

    mu = head_mean(y)
    cen = y - mu
    var = head_mean(cen * cen)
    yn = cen * lax.rsqrt(var + GN_EPS) * gg_ref[...][None] + gb_ref[...][None]
    yn = yn + (c_ref[0] + c_ref[1]) * v_ref[...]
    o_ref[...] = (yn * g_ref[...]).astype(o_ref.dtype)


def _scan_post(y, cb, rkv, gate, gn_g, gn_b, steps, n_heads):
    m = rkv.shape[0]
    rows = rkv.shape[1] // 3
    par = pl.BlockSpec((rows, LANES), lambda i: (0, 0))
    return pl.pallas_call(
        functools.partial(_scan_post_kernel, n_heads=n_heads),
        grid=(m // steps,),
        in_specs=[pl.BlockSpec((2, steps, rows, LANES), lambda i: (0, i, 0, 0)),
                  pl.BlockSpec((2, steps, 1, LANES), lambda i: (0, i, 0, 0)),
                  pl.BlockSpec((steps, rows, LANES), lambda i: (i, 2, 0)),
                  pl.BlockSpec((steps, rows, LANES), lambda i: (i, 0, 0)),
                  par, par],
        out_specs=pl.BlockSpec((steps, rows, LANES), lambda i: (i, 0, 0)),
        out_shape=jax.ShapeDtypeStruct((m, rows, LANES), BF16), name="wkv_post",
        compiler_params=_params(("parallel",)),
    )(y, cb, rkv, gate, gn_g, gn_b)


def _perm_last(w, n_heads):
    lead = w.shape[:-1]
    w = w.reshape(lead + (-1, n_heads, HEAD))
    return jnp.swapaxes(w, -1, -2).reshape(lead + (-1,))


def _state_to_tiles(s, n_heads):
    n = s.shape[0]
    n_grp = LANES // n_heads
    rows = HEAD // n_grp
    s = s.reshape(n, 2, n_heads, rows, n_grp, rows, n_grp)
    return s.transpose(0, 1, 3, 6, 5, 4, 2).reshape(n, 2, rows, HEAD, LANES)


def _tiles_to_state(t, n_heads):
    n = t.shape[0]
    n_grp = LANES // n_heads
    rows = HEAD // n_grp
    t = t.reshape(n, 2, rows, n_grp, rows, n_grp, n_heads)
    return t.transpose(0, 1, 6, 2, 5, 4, 3).reshape(n, 2, n_heads, HEAD, HEAD)


def kernel(x_prompt, x_sample, state_rwkv, c, c_ctx, w_ada, b_ada, norm_g, w_ffn_gate, w_ffn_up,
           w_ffn_down, w_in, w_conv_a, ln_a_g, ln_a_b, w_out_a, w_conv_b, w_out_b, w_shift, k_k,
           k_a, r_k, w0, w_w2, a0, w_a2, w_g2, gn_g, gn_b, w_out_c, w_o, final_g):
    batch, seq, d = x_prompt.shape
    dec_batch, dec_seq, _ = x_sample.shape
    depth = w_ada.shape[0]
    d_ff = w_ffn_gate.shape[-1]
    w_a = w_out_a.shape[1]
    w_b = w_out_b.shape[1]
    w_c = w_out_c.shape[1]
    n_heads = w_c // HEAD
    lora_w = w_w2.shape[2]
    lora_a = w_a2.shape[2]
    lora_g = w_g2.shape[1]
    n_prompt = batch * seq
    m = n_prompt + dec_batch * dec_seq
    assert seq & (seq - 1) == 0 and GRID_W & (GRID_W - 1) == 0 and dec_seq % GRID_W == 0
    assert lora_w == LANES and lora_a == LANES and LANES % n_heads == 0 and w_c % LANES == 0
    rows_c = w_c // LANES

    off_a = 0
    off_b = off_a + 2 * w_a
    off_rkv = off_b + 3 * w_b
    off_lw = off_rkv + 3 * w_c
    off_la = off_lw + 2 * lora_w
    off_lg = off_la + 2 * lora_a
    off_g = off_lg + lora_g
    lg_pad = -(-lora_g // LANES) * LANES

    def grp(i, rows):
        r0 = i * rows
        return jnp.where(r0 < n_prompt, 0, 1 + (r0 - n_prompt) // dec_seq)

    n_cond = 1 + dec_batch
    rpad = -(-n_cond // SUBLANES) * SUBLANES
    cond = jnp.concatenate([c_ctx[None, :], c, jnp.zeros((rpad - n_cond, d), F32)], axis=0)
    mod = _adaln_all(cond, w_ada, b_ada).reshape(depth * rpad * N_MOD, 1, d)

    def mod_row(l, g, idx):
        return (l * rpad + g) * N_MOD + idx

    norm_g3 = norm_g.reshape(depth * 3, 1, d)
    x = jnp.concatenate([x_prompt.reshape(n_prompt, d), x_sample.reshape(m - n_prompt, d)], axis=0)

    tr = seq
    row_gcd = math.gcd(n_prompt, dec_seq)
    tm_mm = _pick(row_gcd, (1024, 512, 256, 128))
    tm_big = _pick(row_gcd, (2048, 1024, 512, 256, 128))

    def mod_block(l, idx, rows, tn):
        return (mod, (None, 1, tn), lambda i, j, k: (mod_row(l, grp(i, rows), idx), 0, j))

    def ffn_half(x, l, half):
        base = 0 if half == 0 else 6
        h = _norm_mod(x, norm_g3, l * 3 + (0 if half == 0 else 2), mod,
                      lambda i, which: mod_row(l, grp(i, tr), base + which), BF16, tr)
        tn = _pick(d_ff, (256, 128))
        wg = (w_ffn_gate, (None, None, d, tn), lambda i, j, k: (l, half, 0, j))
        wu = (w_ffn_up, (None, None, d, tn), lambda i, j, k: (l, half, 0, j))
        (hg,) = _fused_matmul(
            "ffn_up", h, (tm_mm, d), lambda i, j, k: (i, 0), [wg, wu], [], [(d_ff, BF16)],
            (m // tm_mm, d_ff // tn, 1), tm_mm, tn,
            lambda accs, ex: [_silu(accs[0]) * accs[1]])
        tk, tail = _split_k(d_ff, (1536, 1024, 512, 256, 128))
        nk = d_ff // tk
        tn2 = _pick(d, (512, 256, 128))
        wd = (w_ffn_down, (None, None, tk, tn2), lambda i, j, k: (l, half, k, j))
        xt = (x, (tm_big, tn2), lambda i, j, k: (i, j))
        tail_ops = None
        if tail:
            tb = (d_ff - tail) // tail
            tail_ops = (hg, (tm_big, tail), lambda i, j, k: (i, tb),
                        w_ffn_down, (None, None, tail, tn2), lambda i, j, k: (l, half, tb, j))
        (x_new,) = _fused_matmul(
            "ffn_down", hg, (tm_big, tk), lambda i, j, k: (i, k), [wd], [xt, mod_block(l, base + 2, tm_big, tn2)],
            [(d, F32)], (m // tm_big, d // tn2, nk), tm_big, tn2,
            lambda accs, ex: [ex[0] + 0.5 * ex[1] * accs[0]], tail=tail_ops)
        return x_new

    def in_proj(name, h, w_arr, w_lead, col0, n_cols, n_w, tn, epilogue, outs, tm_, tk=None):
        tk = d if tk is None else tk
        assert all((col0 + q * n_cols) % tn == 0 for q in range(n_w))
        ws = [(w_arr, (None,) * len(w_lead) + (tk, tn), functools.partial(
            lambda i, j, k, q: w_lead + (k, (col0 + q * n_cols) // tn + j), q=q)) for q in range(n_w)]
        return _fused_matmul(name, h, (tm_, tk), lambda i, j, k: (i, k), ws, [], outs,
                             (m // tm_, n_cols // tn, d // tk), tm_, tn, epilogue)

    def tile_c(p):
        return _perm_last(p.reshape(-1), n_heads).reshape(rows_c, LANES)

    new_states = []
    for l in range(depth):
        x = ffn_half(x, l, 0)

        h = _norm_mod(x, norm_g3, l * 3 + 1, mod, lambda i, which: mod_row(l, grp(i, tr), 3 + which), BF16, tr)
        tn_s = _pick(w_a, (256, 128))
        (glu,) = in_proj("in_glu", h, w_in, (l,), off_a, w_a, 2, tn_s,
                         lambda accs, ex: [accs[0] * _sigmoid(accs[1])], [(w_a, F32)], tm_mm)
        p_b, bg = in_proj("in_b", h, w_in, (l,), off_b, w_b, 3, tn_s,
                          lambda accs, ex: [accs[2] * accs[0], accs[1]], [(w_b, F32), (w_b, F32)], tm_mm)
        tn_l = _pick(w_c, (512, 256, 128))
        w_rkv = _perm_last(lax.slice_in_dim(w_in[l], off_rkv, off_rkv + 3 * w_c, axis=1), n_heads)
        (zrkv,) = in_proj("in_rkv", h, w_rkv, (), 0, 3 * w_c, 1, tn_l, lambda accs, ex: [accs[0]],
                          [(3 * w_c, F32)], tm_mm)
        (lw,) = in_proj("in_lw", h, w_in, (l,), off_lw, 2 * lora_w, 1, 2 * lora_w,
                        lambda accs, ex: [jnp.tanh(accs[0])], [(2 * lora_w, F32)], tm_mm)
        (la,) = in_proj("in_la", h, w_in, (l,), off_la, 2 * lora_a, 1, 2 * lora_a, lambda accs, ex: [accs[0]],
                        [(2 * lora_a, F32)], tm_mm)
        (lg,) = in_proj("in_lg", h, w_in, (l,), off_lg, lg_pad, 1, lg_pad, lambda accs, ex: [_sigmoid(accs[0])],
                        [(lg_pad, BF16)], tm_mm)
        w_gates = lax.slice_in_dim(w_in[l], off_g, off_g + 3 * d, axis=1)
        (sig_g,) = in_proj("in_gates", h, w_gates, (), 0, 3 * d, 1, tn_l, lambda accs, ex: [_sigmoid(accs[0])],
                           [(3 * d, BF16)], tm_mm)

        feat_a = _conv_a(glu, w_conv_a, ln_a_g, ln_a_b, l, n_prompt, seq, tr)
        feat_b = _conv_b(p_b, bg, w_conv_b, l, n_prompt, seq, dec_seq, tr, _pick(w_b, (1024, 512, 256, 128)))

        rkv = _shift_conv(zrkv, _perm_last(w_shift[l], n_heads), n_prompt, seq, dec_seq, tr,
                          _pick(3 * w_c, (1024, 512, 256, 128)))
        tn_c = _pick(w_c, (1024, 512, 256, 128))
        nj_c = w_c // tn_c

        def lora(name, src, w2, bias, epi):
            wspec = (_perm_last(w2, n_heads), (None, LANES, tn_c), lambda i, j, k: (j // nj_c, 0, j % nj_c))
            bspec = (_perm_last(bias, n_heads).reshape(2, 1, w_c), (None, 1, tn_c),
                     lambda i, j, k: (j // nj_c, 0, j % nj_c))
            (o,) = _fused_matmul(name, src, (tm_mm, LANES), lambda i, j, k: (i, j // nj_c), [wspec], [bspec],
                                 [(2 * w_c, F32)], (m // tm_mm, 2 * nj_c, 1), tm_mm, tn_c, epi)
            return o

        decay = lora("decay", lw, w_w2[l], w0[l],
                     lambda accs, ex: [jnp.exp(-DECAY_SCALE * _sigmoid(ex[0] + accs[0]))])
        lr = lora("lr", la, w_a2[l], a0[l], lambda accs, ex: [_sigmoid(ex[0] + accs[0])])
        w_g2p = jnp.pad(_perm_last(w_g2[l], n_heads), ((0, lg_pad - lora_g), (0, 0)))
        (gate_c,) = _fused_matmul("gate_c", lg, (tm_mm, lg_pad), lambda i, j, k: (i, 0),
                                  [(w_g2p, (lg_pad, tn_c), lambda i, j, k: (0, j))], [], [(w_c, F32)],
                                  (m // tm_mm, nj_c, 1), tm_mm, tn_c, lambda accs, ex: [accs[0]])

        rkv3 = rkv.reshape(m, 3 * rows_c, LANES)
        dec3 = decay.reshape(m, 2 * rows_c, LANES)
        lr3 = lr.reshape(m, 2 * rows_c, LANES)
        kk_t, ka_t, rk_t = tile_c(k_k[l]), tile_c(k_a[l]), tile_c(r_k[l])
        steps = _pick(math.gcd(seq, dec_seq), (64, 32, 16, 8))
        y_raw, cb, s_fin = _wkv_scan(rkv3, dec3, lr3, kk_t, ka_t, rk_t, _state_to_tiles(state_rwkv[:, l], n_heads),
                                     (seq,) * batch + (dec_seq,) * dec_batch, steps, n_heads)
        new_states.append(s_fin[:batch])
        feat_c = _scan_post(y_raw, cb, rkv3, gate_c.reshape(m, rows_c, LANES), tile_c(gn_g[l]), tile_c(gn_b[l]),
                            steps, n_heads).reshape(m, w_c)

        feat = jnp.concatenate([feat_a, feat_b, feat_c], axis=1)
        w_out_c_perm = jnp.swapaxes(w_out_c[l].reshape(n_heads, HEAD, d), 0, 1).reshape(w_c, d)
        tn_m = _pick(d, (512, 256, 128))
        tk_m = _pick(math.gcd(math.gcd(w_a, w_b), w_c), (1024, 512, 256, 128))
        merged = _merge(feat, (w_out_a, w_out_b, w_out_c_perm), l, sig_g, d, tm_big, tn_m, tk_m)
        tk = _pick(d, (1024, 512, 256, 128))
        (x,) = _fused_matmul(
            "w_o", merged, (tm_big, tk), lambda i, j, k: (i, k), [(w_o, (None, tk, tn_m), lambda i, j, k: (l, k, j))],
            [(x, (tm_big, tn_m), lambda i, j, k: (i, j)), mod_block(l, 5, tm_big, tn_m)], [(d, F32)],
            (m // tm_big, d // tn_m, d // tk), tm_big, tn_m, lambda accs, ex: [ex[0] + ex[1] * accs[0]])

        x = ffn_half(x, l, 1)

    y = _norm_mod(x, final_g.reshape(1, 1, d), 0, None, None, F32, tr)
    y_prompt = y[:n_prompt].reshape(batch, seq, d)
    y_sample = y[n_prompt:].reshape(dec_batch, dec_seq, d)
    st = jnp.stack([_tiles_to_state(s, n_heads) for s in new_states], axis=1)
    return (y_prompt, y_sample, st)
```

```python
import functools
import math

import jax
import jax.numpy as jnp
from jax import lax
from jax.experimental import pallas as pl
from jax.experimental.pallas import tpu as pltpu

F32 = jnp.float32
BF16 = jnp.bfloat16

GRID_W = 64
HEAD = 64
N_MOD = 9
RMS_EPS = 1e-6
LN_EPS = 1e-5
GN_EPS = HEAD * 1e-5
DECAY_SCALE = 0.6065306597126334
LANES = 128
SUBLANES = 8
VMEM_LIMIT = 56 * 1024 * 1024


def _pick(n, prefs):
    for p in prefs:
        if n % p == 0:
            return p
    raise ValueError(f"no tile in {prefs} divides {n}")


def _split_k(k, prefs):
    for tk in prefs:
        tail = k % tk
        if k >= tk and (tail == 0 or (tail % LANES == 0 and (k - tail) % tail == 0)):
            return tk, tail
    raise ValueError(f"no K tiling for {k}")


def _params(sem):
    return pltpu.CompilerParams(dimension_semantics=sem, vmem_limit_bytes=VMEM_LIMIT)


def _sigmoid(x):
    return 1.0 / (1.0 + jnp.exp(-x))


def _silu(x):
    return x * _sigmoid(x)


def _mm_kernel(*refs, n_w, n_t, n_e, n_o, nk, epilogue, a_prologue):
    a_ref = refs[0]
    w_refs = refs[1:1 + n_w]
    pos = 1 + n_w
    t_refs = refs[pos:pos + 2 * n_t]
    pos += 2 * n_t
    e_refs = refs[pos:pos + n_e]
    pos += n_e
    o_refs = refs[pos:pos + n_o]
    acc_refs = refs[pos + n_o:]
    a = a_ref[...]
    if a_prologue is not None:
        a = a_prologue(a)
    a = a.astype(BF16)
    prods = [jnp.dot(a, w[...].astype(BF16), preferred_element_type=F32) for w in w_refs]

    def first(ps):
        if n_t:
            tail = jnp.dot(t_refs[0][...].astype(BF16), t_refs[1][...].astype(BF16), preferred_element_type=F32)
            return [ps[0] + tail] + ps[1:]
        return ps

    def finish(accs):
        outs = epilogue(accs, [e[...] for e in e_refs])
        for o, val in zip(o_refs, outs):
            o[...] = val.astype(o.dtype)

    if nk == 1:
        finish(first(prods))
        return
    k = pl.program_id(2)

    @pl.when(k == 0)
    def _():
        for acc, p in zip(acc_refs, first(prods)):
            acc[...] = p

    @pl.when(k > 0)
    def _():
        for acc, p in zip(acc_refs, prods):
            acc[...] += p

    @pl.when(k == nk - 1)
    def _():
        finish([acc[...] for acc in acc_refs])


def _fused_matmul(name, a, a_block, a_map, ws, extras, outs, grid, tm, tn, epilogue, a_prologue=None, tail=None):
    nk = grid[2]
    m_rows = grid[0] * tm
    in_specs = [pl.BlockSpec(a_block, a_map)]
    args = [a]
    operands = list(ws)
    if tail is not None:
        operands += [tail[0:3], tail[3:6]]
    for arr, blk, mp in operands + list(extras):
        in_specs.append(pl.BlockSpec(blk, mp))
        args.append(arr)
    out_specs = [pl.BlockSpec((tm, tn), lambda i, j, k: (i, j)) for _ in outs]
    out_shape = [jax.ShapeDtypeStruct((m_rows, n), dt) for n, dt in outs]
    scratch = [pltpu.VMEM((tm, tn), F32) for _ in ws] if nk > 1 else []
    kern = functools.partial(_mm_kernel, n_w=len(ws), n_t=0 if tail is None else 1, n_e=len(extras),
                             n_o=len(outs), nk=nk, epilogue=epilogue, a_prologue=a_prologue)
    return pl.pallas_call(
        kern, grid=grid, in_specs=in_specs, out_specs=out_specs, out_shape=out_shape,
        scratch_shapes=scratch, name=name,
        compiler_params=_params(("parallel", "parallel", "arbitrary")),
    )(*args)


def _merge_kernel(a_ref, w0, w1, w2, g0, g1, g2, o_ref, acc0, acc1, acc2, *, bounds):
    k = pl.program_id(2)
    lo = 0
    for w_ref, acc, hi in zip((w0, w1, w2), (acc0, acc1, acc2), bounds):
        @pl.when(jnp.logical_and(k >= lo, k < hi))
        def _(w_ref=w_ref, acc=acc, lo=lo):
            p = jnp.dot(a_ref[...], w_ref[...].astype(BF16), preferred_element_type=F32)

            @pl.when(k == lo)
            def _():
                acc[...] = p

            @pl.when(k > lo)
            def _():
                acc[...] += p
        lo = hi

    @pl.when(k == bounds[-1] - 1)
    def _():
        o_ref[...] = (g0[...].astype(F32) * acc0[...] + g1[...].astype(F32) * acc1[...]
                      + g2[...].astype(F32) * acc2[...]).astype(o_ref.dtype)


def _merge(feat, w_outs, l, sig_g, d, tm, tn, tk):
    m = feat.shape[0]
    bounds, lows, acc = [], [], 0
    for w in w_outs:
        assert w.shape[-2] % tk == 0
        lows.append(acc)
        acc += w.shape[-2] // tk
        bounds.append(acc)
    assert acc * tk == feat.shape[1]
    nj = d // tn

    def kmap(q):
        return lambda k: jnp.clip(k - lows[q], 0, bounds[q] - lows[q] - 1)

    w_specs = []
    for q, w in enumerate(w_outs):
        if w.ndim == 3:
            w_specs.append(pl.BlockSpec((None, tk, tn), functools.partial(lambda i, j, k, q: (l, kmap(q)(k), j), q=q)))
        else:
            w_specs.append(pl.BlockSpec((tk, tn), functools.partial(lambda i, j, k, q: (kmap(q)(k), j), q=q)))
    gates = [pl.BlockSpec((tm, tn), functools.partial(lambda i, j, k, q: (i, q * nj + j), q=q)) for q in range(3)]
    return pl.pallas_call(
        functools.partial(_merge_kernel, bounds=tuple(bounds)),
        grid=(m // tm, nj, bounds[-1]),
        in_specs=[pl.BlockSpec((tm, tk), lambda i, j, k: (i, k))] + w_specs + gates,
        out_specs=pl.BlockSpec((tm, tn), lambda i, j, k: (i, j)),
        out_shape=jax.ShapeDtypeStruct((m, d), BF16),
        scratch_shapes=[pltpu.VMEM((tm, tn), F32)] * 3, name="merge",
        compiler_params=_params(("parallel", "parallel", "arbitrary")),
    )(feat, *w_outs, sig_g, sig_g, sig_g)


def _ada_kernel(c_ref, w_ref, b_ref, o_ref):
    c = _silu(c_ref[...]).astype(BF16)
    o_ref[...] = jnp.dot(c, w_ref[...].astype(BF16), preferred_element_type=F32) + b_ref[...]


def _adaln_all(cond, w_ada, b_ada):
    depth, d, n = w_ada.shape
    r = cond.shape[0]
    tn = _pick(n, (512, 256, 128))
    return pl.pallas_call(
        _ada_kernel,
        grid=(depth, n // tn),
        in_specs=[pl.BlockSpec((r, d), lambda l, j: (0, 0)),
                  pl.BlockSpec((None, d, tn), lambda l, j: (l, 0, j)),
                  pl.BlockSpec((None, 1, tn), lambda l, j: (l, 0, j))],
        out_specs=pl.BlockSpec((None, r, tn), lambda l, j: (l, 0, j)),
        out_shape=jax.ShapeDtypeStruct((depth, r, n), F32), name="adaln",
        compiler_params=_params(("parallel", "parallel")),
    )(cond, w_ada, b_ada.reshape(depth, 1, n))


def _norm_kernel(x_ref, g_ref, *rest, modulate):
    if modulate:
        sh_ref, sc_ref, o_ref = rest
    else:
        (o_ref,) = rest
    x = x_ref[...]
    ms = jnp.mean(x * x, axis=-1, keepdims=True)
    y = x * lax.rsqrt(ms + RMS_EPS) * g_ref[...]
    if modulate:
        y = y * (1.0 + sc_ref[...]) + sh_ref[...]
    o_ref[...] = y.astype(o_ref.dtype)


def _norm_mod(x, g, g_idx, mod, mod_idx, out_dtype, tr):
    m, d = x.shape
    in_specs = [pl.BlockSpec((tr, d), lambda i: (i, 0)),
                pl.BlockSpec((None, 1, d), lambda i: (g_idx, 0, 0))]
    args = [x, g]
    if mod is not None:
        in_specs += [pl.BlockSpec((None, 1, d), lambda i: (mod_idx(i, 0), 0, 0)),
                     pl.BlockSpec((None, 1, d), lambda i: (mod_idx(i, 1), 0, 0))]
        args += [mod, mod]
    return pl.pallas_call(
        functools.partial(_norm_kernel, modulate=mod is not None),
        grid=(m // tr,), in_specs=in_specs,
        out_specs=pl.BlockSpec((tr, d), lambda i: (i, 0)),
        out_shape=jax.ShapeDtypeStruct((m, d), out_dtype), name="rmsnorm",
        compiler_params=_params(("parallel",)),
    )(*args)


def _conv_a_kernel(g_ref, w_ref, lg_ref, lb_ref, o_ref, pad_ref, *, n_prompt_tiles, seq, taps, halo):
    rows, ch = g_ref.shape
    half = taps // 2
    seg = jnp.where(pl.program_id(0) < n_prompt_tiles, seq, GRID_W)
    zeros = jnp.zeros((halo, ch), F32)
    pad_ref[pl.ds(0, halo), :] = zeros
    pad_ref[pl.ds(halo + rows, halo), :] = zeros
    pad_ref[pl.ds(halo, rows), :] = g_ref[...]
    pos = lax.broadcasted_iota(jnp.int32, (rows, 1), 0) & (seg - 1)
    acc = jnp.zeros((rows, ch), F32)
    for j in range(taps):
        off = j - half
        valid = jnp.logical_and(pos + off >= 0, pos + off < seg)
        xs = pad_ref[pl.ds(halo + off, rows), :]
        acc = acc + jnp.where(valid, xs, 0.0) * w_ref[pl.ds(j, 1), :]
    mu = jnp.mean(acc, axis=-1, keepdims=True)
    cen = acc - mu
    var = jnp.mean(cen * cen, axis=-1, keepdims=True)
    y = cen * lax.rsqrt(var + LN_EPS) * lg_ref[...] + lb_ref[...]
    o_ref[...] = _silu(y).astype(o_ref.dtype)


def _conv_a(g, w_conv, ln_g, ln_b, l, n_prompt_rows, seq, tr):
    m, ch = g.shape
    taps = w_conv.shape[1]
    halo = 2 * SUBLANES
    assert taps // 2 < halo and tr % seq == 0 and tr % GRID_W == 0
    return pl.pallas_call(
        functools.partial(_conv_a_kernel, n_prompt_tiles=n_prompt_rows // tr, seq=seq, taps=taps, halo=halo),
        grid=(m // tr,),
        in_specs=[pl.BlockSpec((tr, ch), lambda i: (i, 0)),
                  pl.BlockSpec((None, taps, ch), lambda i: (l, 0, 0)),
                  pl.BlockSpec((None, 1, ch), lambda i: (l, 0, 0)),
                  pl.BlockSpec((None, 1, ch), lambda i: (l, 0, 0))],
        out_specs=pl.BlockSpec((tr, ch), lambda i: (i, 0)),
        out_shape=jax.ShapeDtypeStruct((m, ch), BF16),
        scratch_shapes=[pltpu.VMEM((tr + 2 * halo, ch), F32)], name="conv_a",
        compiler_params=_params(("parallel",)),
    )(g, w_conv, ln_g.reshape(-1, 1, ch), ln_b.reshape(-1, 1, ch))


def _shift_rows(x, lo_row, hi_row):
    rows = x.shape[0]
    r = lax.broadcasted_iota(jnp.int32, (rows, 1), 0)
    prev = jnp.where(r == 0, lo_row, pltpu.roll(x, 1, axis=0))
    nxt = jnp.where(r == rows - 1, hi_row, pltpu.roll(x, rows - 1, axis=0))
    return prev, nxt


def _conv_b_kernel(p_ref, lo_ref, hi_ref, bg_ref, w_ref, o_ref, *, n_prompt_tiles, tiles_per_img):
    rows, ch = p_ref.shape
    i = pl.program_id(0)
    p = p_ref[...]
    w0 = w_ref[pl.ds(0, 1), :]
    w1 = w_ref[pl.ds(1, 1), :]
    w2 = w_ref[pl.ds(2, 1), :]

    @pl.when(i < n_prompt_tiles)
    def _():
        zero = jnp.zeros((1, ch), F32)
        prev, nxt = _shift_rows(p, zero, zero)
        o_ref[...] = (bg_ref[...] * (w0 * prev + w1 * p + w2 * nxt)).astype(o_ref.dtype)

    @pl.when(i >= n_prompt_tiles)
    def _():
        t = (i - n_prompt_tiles) % tiles_per_img
        m_lo = jnp.where(t == 0, 0.0, 1.0)
        m_hi = jnp.where(t == tiles_per_img - 1, 0.0, 1.0)
        prev = jnp.concatenate([lo_ref[...] * m_lo, p[:rows - GRID_W]], axis=0)
        nxt = jnp.concatenate([p[GRID_W:], hi_ref[...] * m_hi], axis=0)
        o_ref[...] = (bg_ref[...] * (w0 * prev + w1 * p + w2 * nxt)).astype(o_ref.dtype)


def _conv_b(p, bg, w_conv, l, n_prompt_rows, seq, dec_seq, tr, tc):
    m, ch = p.shape
    assert tr == seq and tr % GRID_W == 0 and dec_seq % tr == 0
    hb = tr // GRID_W
    n_hblocks = m // GRID_W
    return pl.pallas_call(
        functools.partial(_conv_b_kernel, n_prompt_tiles=n_prompt_rows // tr, tiles_per_img=dec_seq // tr),
        grid=(m // tr, ch // tc),
        in_specs=[pl.BlockSpec((tr, tc), lambda i, j: (i, j)),
                  pl.BlockSpec((GRID_W, tc), lambda i, j: (jnp.maximum(i * hb - 1, 0), j)),
                  pl.BlockSpec((GRID_W, tc), lambda i, j: (jnp.minimum((i + 1) * hb, n_hblocks - 1), j)),
                  pl.BlockSpec((tr, tc), lambda i, j: (i, j)),
                  pl.BlockSpec((None, 3, tc), lambda i, j: (l, 0, j))],
        out_specs=pl.BlockSpec((tr, tc), lambda i, j: (i, j)),
        out_shape=jax.ShapeDtypeStruct((m, ch), BF16), name="conv_b",
        compiler_params=_params(("parallel", "parallel")),
    )(p, p, p, bg, w_conv)


def _shift_conv_kernel(x_ref, lo_ref, hi_ref, w_ref, o_ref, *, n_prompt_tiles, tiles_per_seq):
    i = pl.program_id(0)
    t = (i - n_prompt_tiles) % tiles_per_seq
    is_prompt = i < n_prompt_tiles
    m_lo = jnp.where(jnp.logical_or(is_prompt, t == 0), 0.0, 1.0)
    m_hi = jnp.where(jnp.logical_or(is_prompt, t == tiles_per_seq - 1), 0.0, 1.0)
    x = x_ref[...]
    prev, nxt = _shift_rows(x, lo_ref[pl.ds(SUBLANES - 1, 1), :] * m_lo, hi_ref[pl.ds(0, 1), :] * m_hi)
    o_ref[...] = (w_ref[pl.ds(0, 1), :] * prev + w_ref[pl.ds(1, 1), :] * x
                  + w_ref[pl.ds(2, 1), :] * nxt).astype(o_ref.dtype)


def _shift_conv(x, w_conv, n_prompt_rows, seq, dec_seq, tr, tc):
    m, ch = x.shape
    assert tr == seq and dec_seq % tr == 0
    hb = tr // SUBLANES
    n_hblocks = m // SUBLANES
    return pl.pallas_call(
        functools.partial(_shift_conv_kernel, n_prompt_tiles=n_prompt_rows // tr, tiles_per_seq=dec_seq // tr),
        grid=(m // tr, ch // tc),
        in_specs=[pl.BlockSpec((tr, tc), lambda i, j: (i, j)),
                  pl.BlockSpec((SUBLANES, tc), lambda i, j: (jnp.maximum(i * hb - 1, 0), j)),
                  pl.BlockSpec((SUBLANES, tc), lambda i, j: (jnp.minimum((i + 1) * hb, n_hblocks - 1), j)),
                  pl.BlockSpec((3, tc), lambda i, j: (0, j))],
        out_specs=pl.BlockSpec((tr, tc), lambda i, j: (i, j)),
        out_shape=jax.ShapeDtypeStruct((m, ch), F32), name="shift_conv",
        compiler_params=_params(("parallel", "parallel")),
    )(x, x, x, w_conv)


def _allsum_groups(x, width):
    s = width
    while s < LANES:
        x = x + pltpu.roll(x, s, axis=x.ndim - 1)
        s *= 2
    return x


def _key_sums(tiles):
    parts = [t.reshape(HEAD // SUBLANES, SUBLANES, LANES).sum(axis=0) for t in tiles]
    sub = lax.broadcasted_iota(jnp.int32, (SUBLANES, LANES), 0)
    half = SUBLANES // 2
    while half >= 1:
        low = (sub & half) == 0
        n = len(parts) // 2
        parts = [jnp.where(low, parts[u] + pltpu.roll(parts[u], SUBLANES - half, axis=0),
                           parts[u + n] + pltpu.roll(parts[u + n], half, axis=0)) for u in range(n)]
        half //= 2
    return parts[0]


def _scan_kernel(blk_ref, sid_ref, edge_ref, r_ref, k_ref, v_ref, w_ref, a_ref, kk_ref, ka_ref, rk_ref, s0_ref,
                 y_ref, c_ref, sf_ref, s_ref, sa_ref, ekk_ref, ew_ref, eb_ref, ekd_ref, er_ref,
                 *, steps, n_zero_init, n_heads):
    del blk_ref
    d = pl.program_id(0)
    g = pl.program_id(1)
    rows = r_ref.shape[1]
    n_grp = LANES // n_heads
    first = edge_ref[0, g] == 1
    last = edge_ref[1, g] == 1
    from_zero = sid_ref[g] < n_zero_init

    @pl.when(jnp.logical_and(first, from_zero))
    def _():
        s_ref[...] = jnp.zeros_like(s_ref)

    @pl.when(jnp.logical_and(first, jnp.logical_not(from_zero)))
    def _():
        s_ref[...] = s0_ref[...]

    flat = (steps * rows, LANES)
    tile = (steps, rows, LANES)

    def head_sum(z):
        return jnp.sum(_allsum_groups(z.reshape(flat), n_heads).reshape(tile), axis=1, keepdims=True)

    k_t = k_ref[...]
    a_t = a_ref[...]
    r_t = r_ref[...]
    kk = k_t * kk_ref[...][None]
    kk = kk / jnp.maximum(jnp.sqrt(head_sum(kk * kk)), 1e-12)
    kd = k_t * (1.0 + (a_t - 1.0) * ka_ref[...][None])
    c_ref[...] = head_sum(r_t * kd * rk_ref[...][None])
    lane_grp = lax.broadcasted_iota(jnp.int32, flat, 1) // n_heads
    for src, dst in ((kk, ekk_ref), (w_ref[...], ew_ref), (kk * a_t, eb_ref), (kd, ekd_ref), (r_t, er_ref)):
        src = src.reshape(flat)
        shifted = [src] + [pltpu.roll(src, s * n_heads, axis=1) for s in range(1, n_grp)]
        for p in range(n_grp):
            e = shifted[(n_grp - 1 - p) % n_grp]
            for g in range(n_grp - 1):
                e = jnp.where(lane_grp == g, shifted[(g - p) % n_grp], e)
            dst[:, pl.ds(p * rows, rows), :] = e.reshape(tile)

    def step(n, carry):
        tt = jnp.where(d == 0, n, steps - 1 - n)
        e_kk = ekk_ref[tt]
        e_w = ew_ref[tt]
        e_b = eb_ref[tt]
        e_kd = ekd_ref[tt]
        e_r = er_ref[tt]

        def read_out(i, carry2):
            base = pl.multiple_of(i * SUBLANES, SUBLANES)
            sa_ref[pl.ds(base, SUBLANES), :] = _key_sums([s_ref[base + u] * e_kk for u in range(SUBLANES)])
            return carry2

        lax.fori_loop(0, rows // SUBLANES, read_out, 0, unroll=2)

        def update(i, carry2):
            base = pl.multiple_of(i * SUBLANES, SUBLANES)
            prods = []
            for u in range(SUBLANES):
                jv = base + u
                sn = s_ref[jv] * e_w - sa_ref[pl.ds(jv, 1), :] * e_b + v_ref[tt, pl.ds(jv, 1), :] * e_kd
                s_ref[jv] = sn
                prods.append(sn * e_r)
            y_ref[tt, pl.ds(base, SUBLANES), :] = _key_sums(prods)
            return carry2

        lax.fori_loop(0, rows // SUBLANES, update, 0, unroll=2)
        return carry

    lax.fori_loop(0, steps, step, 0)

    @pl.when(last)
    def _():
        sf_ref[...] = s_ref[...]


def _wkv_scan(rkv, dec, lr, k_k, k_a, r_k, s0, seq_lens, steps, n_heads):
    m = rkv.shape[0]
    rows = rkv.shape[1] // 3
    n_seq = len(seq_lens)
    n_zero_init = n_seq - s0.shape[0]
    blk, sid, edge = [[], []], [], [[], []]
    base = 0
    for s, ln in enumerate(seq_lens):
        n = ln // steps
        blk[0] += [base + c for c in range(n)]
        blk[1] += [base + n - 1 - c for c in range(n)]
        sid += [s] * n
        edge[0] += [1] + [0] * (n - 1)
        edge[1] += [0] * (n - 1) + [1]
        base += n
    n_chunks = base
    blk = jnp.asarray(blk, jnp.int32)
    sid = jnp.asarray(sid, jnp.int32)
    edge = jnp.asarray(edge, jnp.int32)

    tile = (steps, rows, LANES)
    par = pl.BlockSpec((rows, LANES), lambda d, g, blk, sid, edge: (0, 0))
    grid_spec = pltpu.PrefetchScalarGridSpec(
        num_scalar_prefetch=3, grid=(2, n_chunks),
        in_specs=[pl.BlockSpec(tile, lambda d, g, blk, sid, edge: (blk[d, g], 0, 0)),
                  pl.BlockSpec(tile, lambda d, g, blk, sid, edge: (blk[d, g], 1, 0)),
                  pl.BlockSpec(tile, lambda d, g, blk, sid, edge: (blk[d, g], 2, 0)),
                  pl.BlockSpec(tile, lambda d, g, blk, sid, edge: (blk[d, g], d, 0)),
                  pl.BlockSpec(tile, lambda d, g, blk, sid, edge: (blk[d, g], d, 0)),
                  par, par, par,
                  pl.BlockSpec((None, None, rows, HEAD, LANES),
                               lambda d, g, blk, sid, edge: (jnp.maximum(sid[g] - n_zero_init, 0), d, 0, 0, 0))],
        out_specs=[pl.BlockSpec((None, steps, rows, LANES), lambda d, g, blk, sid, edge: (d, blk[d, g], 0, 0)),
                   pl.BlockSpec((None, steps, 1, LANES), lambda d, g, blk, sid, edge: (d, blk[d, g], 0, 0)),
                   pl.BlockSpec((None, None, rows, HEAD, LANES),
                                lambda d, g, blk, sid, edge: (sid[g], d, 0, 0, 0))],
        scratch_shapes=[pltpu.VMEM((rows, HEAD, LANES), F32), pltpu.VMEM((rows, LANES), F32)]
        + [pltpu.VMEM((steps, HEAD, LANES), F32)] * 5)
    return pl.pallas_call(
        functools.partial(_scan_kernel, steps=steps, n_zero_init=n_zero_init, n_heads=n_heads),
        grid_spec=grid_spec,
        out_shape=[jax.ShapeDtypeStruct((2, m, rows, LANES), F32),
                   jax.ShapeDtypeStruct((2, m, 1, LANES), F32),
                   jax.ShapeDtypeStruct((n_seq, 2, rows, HEAD, LANES), F32)],
        name="wkv_scan",
        compiler_params=_params(("arbitrary", "arbitrary")),
    )(blk, sid, edge, rkv, rkv, rkv, dec, lr, k_k, k_a, r_k, s0)


def _scan_post_kernel(y_ref, c_ref, v_ref, g_ref, gg_ref, gb_ref, o_ref, *, n_heads):
    steps, rows, _ = v_ref.shape
    y = y_ref[0] + y_ref[1]
    flat = (steps * rows, LANES)

    def head_mean(z):
        tot = _allsum_groups(z.reshape(flat), n_heads).reshape(steps, rows, LANES)
        return jnp.sum(tot, axis=1, keepdims=True) * (1.0 / HEAD)

    mu = head_mean(y)
    cen = y - mu
    var = head_mean(cen * cen)
    yn = cen * lax.rsqrt(var + GN_EPS) * gg_ref[...][None] + gb_ref[...][None]
    yn = yn + (c_ref[0] + c_ref[1]) * v_ref[...]
    o_ref[...] = (yn * g_ref[...]).astype(o_ref.dtype)


def _scan_post(y, cb, rkv, gate, gn_g, gn_b, steps, n_heads):
    m = rkv.shape[0]
    rows = rkv.shape[1] // 3
    par = pl.BlockSpec((rows, LANES), lambda i: (0, 0))
    return pl.pallas_call(
        functools.partial(_scan_post_kernel, n_heads=n_heads),
        grid=(m // steps,),
        in_specs=[pl.BlockSpec((2, steps, rows, LANES), lambda i: (0, i, 0, 0)),
                  pl.BlockSpec((2, steps, 1, LANES), lambda i: (0, i, 0, 0)),
                  pl.BlockSpec((steps, rows, LANES), lambda i: (i, 2, 0)),
                  pl.BlockSpec((steps, rows, LANES), lambda i: (i, 0, 0)),
                  par, par],
        out_specs=pl.BlockSpec((steps, rows, LANES), lambda i: (i, 0, 0)),
        out_shape=jax.ShapeDtypeStruct((m, rows, LANES), BF16), name="wkv_post",
        compiler_params=_params(("parallel",)),
    )(y, cb, rkv, gate, gn_g, gn_b)


def _perm_last(w, n_heads):
    lead = w.shape[:-1]
    w = w.reshape(lead + (-1, n_heads, HEAD))
    return jnp.swapaxes(w, -1, -2).reshape(lead + (-1,))


def _state_to_tiles(s, n_heads):
    n = s.shape[0]
    n_grp = LANES // n_heads
    rows = HEAD // n_grp
    s = s.reshape(n, 2, n_heads, rows, n_grp, rows, n_grp)
    return s.transpose(0, 1, 3, 6, 5, 4, 2).reshape(n, 2, rows, HEAD, LANES)


def _tiles_to_state(t, n_heads):
    n = t.shape[0]
    n_grp = LANES // n_heads
    rows = HEAD // n_grp
    t = t.reshape(n, 2, rows, n_grp, rows, n_grp, n_heads)
    return t.transpose(0, 1, 6, 2, 5, 4, 3).reshape(n, 2, n_heads, HEAD, HEAD)


def kernel(x_prompt, x_sample, state_rwkv, c, c_ctx, w_ada, b_ada, norm_g, w_ffn_gate, w_ffn_up,
           w_ffn_down, w_in, w_conv_a, ln_a_g, ln_a_b, w_out_a, w_conv_b, w_out_b, w_shift, k_k,
           k_a, r_k, w0, w_w2, a0, w_a2, w_g2, gn_g, gn_b, w_out_c, w_o, final_g):
    batch, seq, d = x_prompt.shape
    dec_batch, dec_seq, _ = x_sample.shape
    depth = w_ada.shape[0]
    d_ff = w_ffn_gate.shape[-1]
    w_a = w_out_a.shape[1]
    w_b = w_out_b.shape[1]
    w_c = w_out_c.shape[1]
    n_heads = w_c // HEAD
    lora_w = w_w2.shape[2]
    lora_a = w_a2.shape[2]
    lora_g = w_g2.shape[1]
    n_prompt = batch * seq
    m = n_prompt + dec_batch * dec_seq
    assert seq & (seq - 1) == 0 and GRID_W & (GRID_W - 1) == 0 and dec_seq % GRID_W == 0
    assert lora_w == LANES and lora_a == LANES and LANES % n_heads == 0 and w_c % LANES == 0
    rows_c = w_c // LANES

    off_a = 0
    off_b = off_a + 2 * w_a
    off_rkv = off_b + 3 * w_b
    off_lw = off_rkv + 3 * w_c
    off_la = off_lw + 2 * lora_w
    off_lg = off_la + 2 * lora_a
    off_g = off_lg + lora_g
    lg_pad = -(-lora_g // LANES) * LANES

    def grp(i, rows):
        r0 = i * rows
        return jnp.where(r0 < n_prompt, 0, 1 + (r0 - n_prompt) // dec_seq)

    n_cond = 1 + dec_batch
    rpad = -(-n_cond // SUBLANES) * SUBLANES
    cond = jnp.concatenate([c_ctx[None, :], c, jnp.zeros((rpad - n_cond, d), F32)], axis=0)
    mod = _adaln_all(cond, w_ada, b_ada).reshape(depth * rpad * N_MOD, 1, d)

    def mod_row(l, g, idx):
        return (l * rpad + g) * N_MOD + idx

    norm_g3 = norm_g.reshape(depth * 3, 1, d)
    x = jnp.concatenate([x_prompt.reshape(n_prompt, d), x_sample.reshape(m - n_prompt, d)], axis=0)

    tr = seq
    row_gcd = math.gcd(n_prompt, dec_seq)
    tm_mm = _pick(row_gcd, (1024, 512, 256, 128))
    tm_big = _pick(row_gcd, (2048, 1024, 512, 256, 128))

    def mod_block(l, idx, rows, tn):
        return (mod, (None, 1, tn), lambda i, j, k: (mod_row(l, grp(i, rows), idx), 0, j))

    def ffn_half(x, l, half):
        base = 0 if half == 0 else 6
        h = _norm_mod(x, norm_g3, l * 3 + (0 if half == 0 else 2), mod,
                      lambda i, which: mod_row(l, grp(i, tr), base + which), BF16, tr)
        tn = _pick(d_ff, (256, 128))
        wg = (w_ffn_gate, (None, None, d, tn), lambda i, j, k: (l, half, 0, j))
        wu = (w_ffn_up, (None, None, d, tn), lambda i, j, k: (l, half, 0, j))
        (hg,) = _fused_matmul(
            "ffn_up", h, (tm_mm, d), lambda i, j, k: (i, 0), [wg, wu], [], [(d_ff, BF16)],
            (m // tm_mm, d_ff // tn, 1), tm_mm, tn,
            lambda accs, ex: [_silu(accs[0]) * accs[1]])
        tk, tail = _split_k(d_ff, (1536, 1024, 512, 256, 128))
        nk = d_ff // tk
        tn2 = _pick(d, (512, 256, 128))
        wd = (w_ffn_down, (None, None, tk, tn2), lambda i, j, k: (l, half, k, j))
        xt = (x, (tm_big, tn2), lambda i, j, k: (i, j))
        tail_ops = None
        if tail:
            tb = (d_ff - tail) // tail
            tail_ops = (hg, (tm_big, tail), lambda i, j, k: (i, tb),
                        w_ffn_down, (None, None, tail, tn2), lambda i, j, k: (l, half, tb, j))
        (x_new,) = _fused_matmul(
            "ffn_down", hg, (tm_big, tk), lambda i, j, k: (i, k), [wd], [xt, mod_block(l, base + 2, tm_big, tn2)],
            [(d, F32)], (m // tm_big, d // tn2, nk), tm_big, tn2,
            lambda accs, ex: [ex[0] + 0.5 * ex[1] * accs[0]], tail=tail_ops)
        return x_new

    def in_proj(name, h, w_arr, w_lead, col0, n_cols, n_w, tn, epilogue, outs, tm_, tk=None):
        tk = d if tk is None else tk
        assert all((col0 + q * n_cols) % tn == 0 for q in range(n_w))
        ws = [(w_arr, (None,) * len(w_lead) + (tk, tn), functools.partial(
            lambda i, j, k, q: w_lead + (k, (col0 + q * n_cols) // tn + j), q=q)) for q in range(n_w)]
        return _fused_matmul(name, h, (tm_, tk), lambda i, j, k: (i, k), ws, [], outs,
                             (m // tm_, n_cols // tn, d // tk), tm_, tn, epilogue)

    def tile_c(p):
        return _perm_last(p.reshape(-1), n_heads).reshape(rows_c, LANES)

    new_states = []
    for l in range(depth):
        x = ffn_half(x, l, 0)

        h = _norm_mod(x, norm_g3, l * 3 + 1, mod, lambda i, which: mod_row(l, grp(i, tr), 3 + which), BF16, tr)
        tn_s = _pick(w_a, (256, 128))
        (glu,) = in_proj("in_glu", h, w_in, (l,), off_a, w_a, 2, tn_s,
                         lambda accs, ex: [accs[0] * _sigmoid(accs[1])], [(w_a, F32)], tm_mm)
        p_b, bg = in_proj("in_b", h, w_in, (l,), off_b, w_b, 3, tn_s,
                          lambda accs, ex: [accs[2] * accs[0], accs[1]], [(w_b, F32), (w_b, F32)], tm_mm)
        tn_l = _pick(w_c, (512, 256, 128))
        w_rkv = _perm_last(lax.slice_in_dim(w_in[l], off_rkv, off_rkv + 3 * w_c, axis=1), n_heads)
        (zrkv,) = in_proj("in_rkv", h, w_rkv, (), 0, 3 * w_c, 1, tn_l, lambda accs, ex: [accs[0]],
                          [(3 * w_c, F32)], tm_mm)
        (lw,) = in_proj("in_lw", h, w_in, (l,), off_lw, 2 * lora_w, 1, 2 * lora_w,
                        lambda accs, ex: [jnp.tanh(accs[0])], [(2 * lora_w, F32)], tm_mm)
        (la,) = in_proj("in_la", h, w_in, (l,), off_la, 2 * lora_a, 1, 2 * lora_a, lambda accs, ex: [accs[0]],
                        [(2 * lora_a, F32)], tm_mm)
        (lg,) = in_proj("in_lg", h, w_in, (l,), off_lg, lg_pad, 1, lg_pad, lambda accs, ex: [_sigmoid(accs[0])],
                        [(lg_pad, BF16)], tm_mm)
        w_gates = lax.slice_in_dim(w_in[l], off_g, off_g + 3 * d, axis=1)
        (sig_g,) = in_proj("in_gates", h, w_gates, (), 0, 3 * d, 1, tn_l, lambda accs, ex: [_sigmoid(accs[0])],
                           [(3 * d, BF16)], tm_mm)

        feat_a = _conv_a(glu, w_conv_a, ln_a_g, ln_a_b, l, n_prompt, seq, tr)
        feat_b = _conv_b(p_b, bg, w_conv_b, l, n_prompt, seq, dec_seq, tr, _pick(w_b, (1024, 512, 256, 128)))

        rkv = _shift_conv(zrkv, _perm_last(w_shift[l], n_heads), n_prompt, seq, dec_seq, tr,
                          _pick(3 * w_c, (1024, 512, 256, 128)))
        tn_c = _pick(w_c, (1024, 512, 256, 128))
        nj_c = w_c // tn_c

        def lora(name, src, w2, bias, epi):
            wspec = (_perm_last(w2, n_heads), (None, LANES, tn_c), lambda i, j, k: (j // nj_c, 0, j % nj_c))
            bspec = (_perm_last(bias, n_heads).reshape(2, 1, w_c), (None, 1, tn_c),
                     lambda i, j, k: (j // nj_c, 0, j % nj_c))
            (o,) = _fused_matmul(name, src, (tm_mm, LANES), lambda i, j, k: (i, j // nj_c), [wspec], [bspec],
                                 [(2 * w_c, F32)], (m // tm_mm, 2 * nj_c, 1), tm_mm, tn_c, epi)
            return o

        decay = lora("decay", lw, w_w2[l], w0[l],
                     lambda accs, ex: [jnp.exp(-DECAY_SCALE * _sigmoid(ex[0] + accs[0]))])
        lr = lora("lr", la, w_a2[l], a0[l], lambda accs, ex: [_sigmoid(ex[0] + accs[0])])
        w_g2p = jnp.pad(_perm_last(w_g2[l], n_heads), ((0, lg_pad - lora_g), (0, 0)))
        (gate_c,) = _fused_matmul("gate_c", lg, (tm_mm, lg_pad), lambda i, j, k: (i, 0),
                                  [(w_g2p, (lg_pad, tn_c), lambda i, j, k: (0, j))], [], [(w_c, F32)],
                                  (m // tm_mm, nj_c, 1), tm_mm, tn_c, lambda accs, ex: [accs[0]])

        rkv3 = rkv.reshape(m, 3 * rows_c, LANES)
        dec3 = decay.reshape(m, 2 * rows_c, LANES)
        lr3 = lr.reshape(m, 2 * rows_c, LANES)
        kk_t, ka_t, rk_t = tile_c(k_k[l]), tile_c(k_a[l]), tile_c(r_k[l])
        steps = _pick(math.gcd(seq, dec_seq), (64, 32, 16, 8))
        y_raw, cb, s_fin = _wkv_scan(rkv3, dec3, lr3, kk_t, ka_t, rk_t, _state_to_tiles(state_rwkv[:, l], n_heads),
                                     (seq,) * batch + (dec_seq,) * dec_batch, steps, n_heads)
        new_states.append(s_fin[:batch])
        feat_c = _scan_post(y_raw, cb, rkv3, gate_c.reshape(m, rows_c, LANES), tile_c(gn_g[l]), tile_c(gn_b[l]),
                            steps, n_heads).reshape(m, w_c)

        feat = jnp.concatenate([feat_a, feat_b, feat_c], axis=1)
        w_out_c_perm = jnp.swapaxes(w_out_c[l].reshape(n_heads, HEAD, d), 0, 1).reshape(w_c, d)
        tn_m = _pick(d, (512, 256, 128))
        tk_m = _pick(math.gcd(math.gcd(w_a, w_b), w_c), (1024, 512, 256, 128))
        merged = _merge(feat, (w_out_a, w_out_b, w_out_c_perm), l, sig_g, d, tm_big, tn_m, tk_m)
        tk = _pick(d, (1024, 512, 256, 128))
        (x,) = _fused_matmul(
            "w_o", merged, (tm_big, tk), lambda i, j, k: (i, k), [(w_o, (None, tk, tn_m), lambda i, j, k: (l, k, j))],
            [(x, (tm_big, tn_m), lambda i, j, k: (i, j)), mod_block(l, 5, tm_big, tn_m)], [(d, F32)],
            (m // tm_big, d // tn_m, d // tk), tm_big, tn_m, lambda accs, ex: [ex[0] + ex[1] * accs[0]])

        x = ffn_half(x, l, 1)

    y = _norm_mod(x, final_g.reshape(1, 1, d), 0, None, None, F32, tr)
    y_prompt = y[:n_prompt].reshape(batch, seq, d)
    y_sample = y[n_prompt:].reshape(dec_batch, dec_seq, d)
    st = jnp.stack([_tiles_to_state(s, n_heads) for s in new_states], axis=1)
    return (y_prompt, y_sample, st)
```
